```python
import math
import numpy as np
import jax
import jax.numpy as jnp
from jax import lax

D_MODEL = 1024
BATCH = 32
SEQ = 2048
DEPTH = 2

PLE_DIM = 256
CONV_WIDTH = 4
RMS_EPS = 1e-6
DN_HEADS = 6
DN_HEAD_DIM = 128
DN_WIDTH = DN_HEADS * DN_HEAD_DIM
DN_CHUNK = 64
SSM_HEADS = 12
SSM_HEAD_DIM = 64
SSM_WIDTH = SSM_HEADS * SSM_HEAD_DIM
SSM_GROUPS = 2
SSM_STATE = 128
SSM_CHUNK = 64
SSM_XBC = SSM_WIDTH + 2 * SSM_GROUPS * SSM_STATE
ATTN_HEADS = 12
ATTN_HEAD_DIM = 64
DILATION_GROUPS = ((128, 1), (512, 4), (2048, 16))
ATTN_GROUP_HEADS = ATTN_HEADS // len(DILATION_GROUPS)
ATTN_OUT_WIDTH = ATTN_GROUP_HEADS * ATTN_HEAD_DIM
ATTN_BLOCK = 128
ALIBI_MAX_BIAS = 8.0
N_BRANCHES = 3
FFN_DIM = 2816
N_EXPERTS = 8
TOP_K = 2
EXPERT_DIM = 3584
IN_SPLIT_SIZES = (3 * DN_WIDTH, DN_WIDTH, DN_HEADS, DN_HEADS, SSM_XBC, SSM_WIDTH, SSM_HEADS, 3 * ATTN_HEADS * ATTN_HEAD_DIM, N_BRANCHES * D_MODEL)
IN_WIDTH = sum(IN_SPLIT_SIZES)

kernel_name = 'hybrid_deltanet_ssd_dilated_attn_moe'

F32 = jnp.float32


def rmsnorm(x, g):
    xf = x.astype(F32)
    y = xf * lax.rsqrt(jnp.mean(xf * xf, axis=-1, keepdims=True) + RMS_EPS)
    return (y * g.astype(F32)).astype(x.dtype)


def l2norm(x):
    xf = x.astype(F32)
    return xf * lax.rsqrt(jnp.sum(xf * xf, axis=-1, keepdims=True) + RMS_EPS)


def causal_conv(x, w):
    k_w = w.shape[0]
    s = x.shape[1]
    xp = jnp.pad(x, ((0, 0), (k_w - 1, 0), (0, 0)))
    y = xp[:, k_w - 1:k_w - 1 + s] * w[k_w - 1]
    for j in range(k_w - 1):
        y = y + xp[:, j:j + s] * w[j]
    return y


def swiglu(h, w_gate, w_up, w_down):
    return (jax.nn.silu(h @ w_gate) * (h @ w_up)) @ w_down


def chunked_gated_delta_rule(q, k, v, g, beta):
    bsz, s, h, dk = q.shape
    dv = v.shape[-1]
    c = DN_CHUNK
    n = s // c

    def to_chunks(t):
        return jnp.moveaxis(t.astype(F32).reshape(bsz, n, c, h, *t.shape[3:]), 3, 2)

    q, k, v, g, beta = [to_chunks(t) for t in (q, k, v, g, beta)]
    q = q * dk ** -0.5
    gam = jnp.cumsum(g, axis=-1)
    incl = jnp.tril(jnp.ones((c, c), bool))
    strict = jnp.tril(jnp.ones((c, c), bool), k=-1)
    diff = gam[..., :, None] - gam[..., None, :]
    decay = jnp.where(incl, jnp.exp(jnp.where(incl, diff, 0.0)), 0.0)
    kb = k * beta[..., None]
    a_mat = jnp.where(strict, jnp.einsum('bnhid,bnhjd->bnhij', kb, k) * decay, 0.0)
    lower = a_mat + jnp.eye(c, dtype=F32)
    rhs = jnp.concatenate([kb * jnp.exp(gam)[..., None], v * beta[..., None]], axis=-1)
    sol = lax.linalg.triangular_solve(lower, rhs, left_side=True, lower=True, unit_diagonal=True)
    w_c, u_c = sol[..., :dk], sol[..., dk:]
    q_dec = q * jnp.exp(gam)[..., None]
    attn_qk = jnp.einsum('bnhid,bnhjd->bnhij', q, k) * decay
    g_last = gam[..., -1]
    k_dec = k * jnp.exp(g_last[..., None] - gam)[..., None]

    def step(state, xs):
        w_i, u_i, qd_i, aqk_i, kd_i, gl_i = xs
        v_new = u_i - jnp.einsum('bhcd,bhde->bhce', w_i, state)
        o_i = jnp.einsum('bhcd,bhde->bhce', qd_i, state) + jnp.einsum('bhij,bhje->bhie', aqk_i, v_new)
        state = state * jnp.exp(gl_i)[..., None, None] + jnp.einsum('bhcd,bhce->bhde', kd_i, v_new)
        return state, o_i

    xs = tuple(jnp.moveaxis(t, 1, 0) for t in (w_c, u_c, q_dec, attn_qk, k_dec, g_last))
    _, o = lax.scan(step, jnp.zeros((bsz, h, dk, dv), F32), xs)
    o = jnp.moveaxis(jnp.moveaxis(o, 0, 1), 3, 2)
    return o.reshape(bsz, s, h, dv)


def gated_deltanet(qkv, z, a, b, conv_w, a_log, dt_bias, norm_g):
    bsz, s, _ = qkv.shape
    qkv = jax.nn.silu(causal_conv(qkv, conv_w))
    q, k, v = jnp.split(qkv, 3, axis=-1)
    q = l2norm(q.reshape(bsz, s, DN_HEADS, DN_HEAD_DIM))
    k = l2norm(k.reshape(bsz, s, DN_HEADS, DN_HEAD_DIM))
    v = v.reshape(bsz, s, DN_HEADS, DN_HEAD_DIM)
    g = -jnp.exp(a_log.astype(F32)) * jax.nn.softplus(a.astype(F32) + dt_bias.astype(F32))
    beta = jax.nn.sigmoid(b.astype(F32))
    o = chunked_gated_delta_rule(q, k, v, g, beta)
    o = rmsnorm(o, norm_g) * jax.nn.silu(z.reshape(bsz, s, DN_HEADS, DN_HEAD_DIM).astype(F32))
    return o.reshape(bsz, s, DN_WIDTH).astype(qkv.dtype)


def ssd_chunked(x, dt, a_neg, bm, cm):
    bsz, s, h, pd = x.shape
    grp, ns = bm.shape[2], bm.shape[3]
    hpg = h // grp
    c = SSM_CHUNK
    nc = s // c
    xdt = (x.astype(F32) * dt[..., None]).reshape(bsz, nc, c, grp, hpg, pd)
    a = jnp.moveaxis((dt * a_neg.astype(F32)).reshape(bsz, nc, c, grp, hpg), 2, -1)
    bc = bm.astype(F32).reshape(bsz, nc, c, grp, ns)
    cc = cm.astype(F32).reshape(bsz, nc, c, grp, ns)
    acs = jnp.cumsum(a, axis=-1)
    incl = jnp.tril(jnp.ones((c, c), bool))
    diff = acs[..., :, None] - acs[..., None, :]
    lmat = jnp.where(incl, jnp.exp(jnp.where(incl, diff, 0.0)), 0.0)
    cb = jnp.einsum('bcigs,bcjgs->bcgij', cc, bc)
    y_diag = jnp.einsum('bcgij,bcghij,bcjghp->bcighp', cb, lmat, xdt)
    decay_end = jnp.exp(acs[..., -1:] - acs)
    states = jnp.einsum('bcjgs,bcghj,bcjghp->bcghps', bc, decay_end, xdt)
    chunk_decay = jnp.exp(acs[..., -1])

    def step(state, xs):
        st_i, dec_i = xs
        return state * dec_i[..., None, None] + st_i, state

    _, prev = lax.scan(step, jnp.zeros((bsz, grp, hpg, pd, ns), F32), (jnp.moveaxis(states, 1, 0), jnp.moveaxis(chunk_decay, 1, 0)))
    prev = jnp.moveaxis(prev, 0, 1)
    y_off = jnp.einsum('bcigs,bcghps,bcghi->bcighp', cc, prev, jnp.exp(acs))
    return (y_diag + y_off).reshape(bsz, s, h, pd)


def mamba2_ssd(xbc, z, dt, conv_w, conv_b, a_log, dt_bias, d_skip, norm_g):
    bsz, s, _ = xbc.shape
    xbc = jax.nn.silu(causal_conv(xbc, conv_w) + conv_b)
    xs, bm, cm = jnp.split(xbc, [SSM_WIDTH, SSM_WIDTH + SSM_GROUPS * SSM_STATE], axis=-1)
    x = xs.reshape(bsz, s, SSM_HEADS, SSM_HEAD_DIM)
    bm = bm.reshape(bsz, s, SSM_GROUPS, SSM_STATE)
    cm = cm.reshape(bsz, s, SSM_GROUPS, SSM_STATE)
    dt = jax.nn.softplus(dt.astype(F32) + dt_bias.astype(F32))
    y = ssd_chunked(x, dt, -jnp.exp(a_log.astype(F32)), bm, cm) + d_skip.astype(F32)[:, None] * x.astype(F32)
    y = y.reshape(bsz, s, SSM_GROUPS, SSM_WIDTH // SSM_GROUPS) * jax.nn.silu(z.reshape(bsz, s, SSM_GROUPS, SSM_WIDTH // SSM_GROUPS).astype(F32))
    y = rmsnorm(y, norm_g.reshape(SSM_GROUPS, SSM_WIDTH // SSM_GROUPS))
    return y.reshape(bsz, s, SSM_WIDTH).astype(xbc.dtype)


def alibi_slopes(n_heads):
    return jnp.exp2(-ALIBI_MAX_BIAS * (jnp.arange(n_heads, dtype=F32) + 1.0) / n_heads)


def dilated_window_attention(q, k, v, slopes, window, dil):
    bsz, s, hg, e = q.shape
    sub_len = s // dil
    w_sub = window // dil
    c = ATTN_BLOCK
    nb = -(-sub_len // c)
    lp = nb * c

    def to_sub(t):
        return t.reshape(bsz, sub_len, dil, hg, e).transpose(0, 2, 3, 1, 4)

    qs, ks, vs = to_sub(q), to_sub(k), to_sub(v)
    qb = jnp.pad(qs, ((0, 0), (0, 0), (0, 0), (0, lp - sub_len), (0, 0))).reshape(bsz, dil, hg, nb, c, e)

    def band(t):
        tp = jnp.pad(t, ((0, 0), (0, 0), (0, 0), (c, lp - sub_len), (0, 0))).reshape(bsz, dil, hg, nb + 1, c, e)
        return jnp.concatenate([tp[:, :, :, :-1], tp[:, :, :, 1:]], axis=-2)

    kb, vb = band(ks), band(vs)
    scores = jnp.einsum('bdhnqe,bdhnke->bdhnqk', qb, kb).astype(F32) * e ** -0.5
    qpos = jnp.arange(c)[:, None]
    kpos = jnp.arange(2 * c)[None, :]
    delta = c + qpos - kpos
    key_idx = (jnp.arange(nb)[:, None, None] - 1) * c + kpos[None]
    valid = (delta >= 0) & (delta <= w_sub) & (key_idx >= 0)
    bias = -slopes.astype(F32)[:, None, None] * (delta * dil).astype(F32)
    scores = jnp.where(valid, scores + bias[:, None], -jnp.inf)
    m = jnp.max(scores, axis=-1, keepdims=True)
    ex = jnp.exp(scores - m)
    den = jnp.sum(ex, axis=-1, keepdims=True)
    o = jnp.einsum('bdhnqk,bdhnke->bdhnqe', (ex / den).astype(v.dtype), vb)
    lse = (m + jnp.log(den))[..., 0]
    o = o.reshape(bsz, dil, hg, lp, e)[:, :, :, :sub_len].transpose(0, 3, 1, 2, 4).reshape(bsz, s, hg, e)
    lse = lse.reshape(bsz, dil, hg, lp)[..., :sub_len].transpose(0, 3, 1, 2).reshape(bsz, s, hg)
    return o, lse


def dilated_attention(qkv):
    bsz, s, _ = qkv.shape
    qkv = qkv.reshape(bsz, s, 3, ATTN_HEADS, ATTN_HEAD_DIM)
    q, k, v = qkv[:, :, 0], qkv[:, :, 1], qkv[:, :, 2]
    slopes = alibi_slopes(ATTN_HEADS)
    outs, lses = [], []
    for gi, (window, dil) in enumerate(DILATION_GROUPS):
        hs = slice(gi * ATTN_GROUP_HEADS, (gi + 1) * ATTN_GROUP_HEADS)
        o, lse = dilated_window_attention(q[:, :, hs], k[:, :, hs], v[:, :, hs], slopes[hs], window, dil)
        outs.append(o)
        lses.append(lse)
    wts = jax.nn.softmax(jnp.stack(lses, axis=0), axis=0)
    o = jnp.sum(wts[..., None] * jnp.stack(outs, axis=0).astype(F32), axis=0)
    return o.reshape(bsz, s, ATTN_OUT_WIDTH).astype(qkv.dtype)


def hybrid_mixer(h, w_in, dn_conv, dn_a_log, dn_dt_bias, dn_norm, ssm_conv, ssm_conv_b, ssm_a_log, ssm_dt_bias, ssm_d, ssm_norm, w_br_dn, w_br_ssm, w_br_attn, w_out):
    cuts = np.cumsum(IN_SPLIT_SIZES)[:-1].tolist()
    dn_qkv, dn_z, dn_a, dn_b, ssm_xbc, ssm_z, ssm_dt, attn_qkv, gate_logits = jnp.split(h @ w_in, cuts, axis=-1)
    y_dn = gated_deltanet(dn_qkv, dn_z, dn_a, dn_b, dn_conv, dn_a_log, dn_dt_bias, dn_norm)
    y_ssm = mamba2_ssd(ssm_xbc, ssm_z, ssm_dt, ssm_conv, ssm_conv_b, ssm_a_log, ssm_dt_bias, ssm_d, ssm_norm)
    y_attn = dilated_attention(attn_qkv)
    g_dn, g_ssm, g_attn = jnp.split(jax.nn.sigmoid(gate_logits), N_BRANCHES, axis=-1)
    merged = g_dn * (y_dn @ w_br_dn) + g_ssm * (y_ssm @ w_br_ssm) + g_attn * (y_attn @ w_br_attn)
    return merged @ w_out


def moe_swiglu(h, w_router, w_gate, w_up, w_down):
    logits = (h @ w_router).astype(F32)
    top_vals, top_idx = lax.top_k(logits, TOP_K)
    top_w = jax.nn.softmax(top_vals, axis=-1)
    gates = jnp.sum(jax.nn.one_hot(top_idx, N_EXPERTS, dtype=F32) * top_w[..., None], axis=-2).astype(h.dtype)
    out = gates[..., 0:1] * swiglu(h, w_gate[0], w_up[0], w_down[0])
    for e in range(1, N_EXPERTS):
        out = out + gates[..., e:e + 1] * swiglu(h, w_gate[e], w_up[e], w_down[e])
    return out


def setup_inputs(seed: int = 0) -> dict:
    key = jax.random.key(seed)
    keys = list(jax.random.split(key, 40))
    n_dense = (DEPTH + 1) // 2
    n_moe = DEPTH // 2

    def nxt():
        return keys.pop()

    def nrm(shape, fan_in):
        return jax.random.normal(nxt(), shape, F32) * fan_in ** -0.5

    def gain(shape):
        return 1.0 + 0.1 * jax.random.normal(nxt(), shape, F32)

    def log_a(shape):
        return jnp.log(jax.random.uniform(nxt(), shape, F32, 1.0, 16.0))

    def dt_bias(shape):
        dt = jnp.exp(jax.random.uniform(nxt(), shape, F32, math.log(1e-3), math.log(1e-1)))
        return dt + jnp.log(-jnp.expm1(-dt))

    return {
        'x': jax.random.normal(nxt(), (BATCH, SEQ, D_MODEL), F32),
        'p': jax.random.normal(nxt(), (DEPTH, BATCH, SEQ, PLE_DIM), F32),
        'mix_norm': gain((DEPTH, D_MODEL)),
        'w_in': nrm((DEPTH, D_MODEL, IN_WIDTH), D_MODEL),
        'dn_conv': nrm((DEPTH, CONV_WIDTH, 3 * DN_WIDTH), CONV_WIDTH),
        'dn_a_log': log_a((DEPTH, DN_HEADS)),
        'dn_dt_bias': dt_bias((DEPTH, DN_HEADS)),
        'dn_norm': gain((DEPTH, DN_HEAD_DIM)),
        'ssm_conv': nrm((DEPTH, CONV_WIDTH, SSM_XBC), CONV_WIDTH),
        'ssm_conv_b': 0.01 * jax.random.normal(nxt(), (DEPTH, SSM_XBC), F32),
        'ssm_a_log': log_a((DEPTH, SSM_HEADS)),
        'ssm_dt_bias': dt_bias((DEPTH, SSM_HEADS)),
        'ssm_d': gain((DEPTH, SSM_HEADS)),
        'ssm_norm': gain((DEPTH, SSM_WIDTH)),
        'w_br_dn': nrm((DEPTH, DN_WIDTH, D_MODEL), DN_WIDTH),
        'w_br_ssm': nrm((DEPTH, SSM_WIDTH, D_MODEL), SSM_WIDTH),
        'w_br_attn': nrm((DEPTH, ATTN_OUT_WIDTH, D_MODEL), ATTN_OUT_WIDTH),
        'w_out': nrm((DEPTH, D_MODEL, D_MODEL), D_MODEL),
        'ffn_norm': gain((DEPTH, D_MODEL)),
        'w_ff_gate': nrm((n_dense, D_MODEL, FFN_DIM), D_MODEL),
        'w_ff_up': nrm((n_dense, D_MODEL, FFN_DIM), D_MODEL),
        'w_ff_down': nrm((n_dense, FFN_DIM, D_MODEL), FFN_DIM),
        'w_router': nrm((n_moe, D_MODEL, N_EXPERTS), D_MODEL),
        'w_moe_gate': nrm((n_moe, N_EXPERTS, D_MODEL, EXPERT_DIM), D_MODEL),
        'w_moe_up': nrm((n_moe, N_EXPERTS, D_MODEL, EXPERT_DIM), D_MODEL),
        'w_moe_down': nrm((n_moe, N_EXPERTS, EXPERT_DIM, D_MODEL), EXPERT_DIM),
        'ple_norm': gain((DEPTH, D_MODEL)),
        'w_ple': nrm((DEPTH, PLE_DIM, D_MODEL), PLE_DIM),
        'w_ple_gate': nrm((DEPTH, D_MODEL, D_MODEL), D_MODEL),
        'final_norm': gain((D_MODEL,)),
    }


def reference(x, p, mix_norm, w_in, dn_conv, dn_a_log, dn_dt_bias, dn_norm, ssm_conv, ssm_conv_b, ssm_a_log, ssm_dt_bias, ssm_d, ssm_norm, w_br_dn, w_br_ssm, w_br_attn, w_out, ffn_norm, w_ff_gate, w_ff_up, w_ff_down, w_router, w_moe_gate, w_moe_up, w_moe_down, ple_norm, w_ple, w_ple_gate, final_norm):
    for i in range(DEPTH):
        h = rmsnorm(x, mix_norm[i])
        x = x + hybrid_mixer(h, w_in[i], dn_conv[i], dn_a_log[i], dn_dt_bias[i], dn_norm[i], ssm_conv[i], ssm_conv_b[i], ssm_a_log[i], ssm_dt_bias[i], ssm_d[i], ssm_norm[i], w_br_dn[i], w_br_ssm[i], w_br_attn[i], w_out[i])
        h = rmsnorm(x, ffn_norm[i])
        j = i // 2
        if i % 2 == 0:
            x = x + swiglu(h, w_ff_gate[j], w_ff_up[j], w_ff_down[j])
        else:
            x = x + moe_swiglu(h, w_router[j], w_moe_gate[j], w_moe_up[j], w_moe_down[j])
        ple_gate = jax.nn.sigmoid(rmsnorm(x, ple_norm[i]) @ w_ple_gate[i])
        x = x + (p[i] @ w_ple[i]) * ple_gate
    return rmsnorm(x, final_norm)
```

```python
import functools

import numpy as np
import jax
import jax.numpy as jnp
from jax import lax
from jax.experimental import pallas as pl
from jax.experimental.pallas import tpu as pltpu

F32 = jnp.float32
BF16 = jnp.bfloat16

D_MODEL = 1024
DEPTH = 2
PLE_DIM = 256
CONV_WIDTH = 4
RMS_EPS = 1e-6
DN_HEADS = 6
DN_HEAD_DIM = 128
DN_WIDTH = DN_HEADS * DN_HEAD_DIM
DN_CHUNK = 64
SSM_HEADS = 12
SSM_HEAD_DIM = 64
SSM_WIDTH = SSM_HEADS * SSM_HEAD_DIM
SSM_GROUPS = 2
SSM_STATE = 128
SSM_CHUNK = 64
SSM_XBC = SSM_WIDTH + 2 * SSM_GROUPS * SSM_STATE
ATTN_HEADS = 12
ATTN_HEAD_DIM = 64
DILATION_GROUPS = ((128, 1), (512, 4), (2048, 16))
ATTN_GROUP_HEADS = ATTN_HEADS // len(DILATION_GROUPS)
ATTN_OUT_WIDTH = ATTN_GROUP_HEADS * ATTN_HEAD_DIM
ATTN_BLOCK = 128
ALIBI_MAX_BIAS = 8.0
N_BRANCHES = 3
FFN_DIM = 2816
N_EXPERTS = 8
TOP_K = 2
EXPERT_DIM = 3584
IN_SPLIT_SIZES = (3 * DN_WIDTH, DN_WIDTH, DN_HEADS, DN_HEADS, SSM_XBC, SSM_WIDTH, SSM_HEADS,
                  3 * ATTN_HEADS * ATTN_HEAD_DIM, N_BRANCHES * D_MODEL)

LANES = 128
NEG_BIG = -1e30
VMEM_LIMIT_BYTES = 56 * 1024 * 1024


def _cparams(n_axes):
    return pltpu.CompilerParams(dimension_semantics=("arbitrary",) * n_axes, vmem_limit_bytes=VMEM_LIMIT_BYTES)


def _rms(x, g):
    return x * lax.rsqrt(jnp.mean(x * x, axis=-1, keepdims=True) + RMS_EPS) * g


def _dot(a, b):
    return jnp.dot(a, b, preferred_element_type=F32)


def _norm_matmul_body(x_ref, g_ref, w_ref, *rest, chunk, with_small):
    if with_small:
        ws_ref, o_ref, os_ref = rest
    else:
        (o_ref,) = rest
    hn = _rms(x_ref[...], g_ref[...]).astype(BF16)
    for c in range(w_ref.shape[1] // chunk):
        sl = slice(c * chunk, (c + 1) * chunk)
        o_ref[:, sl] = _dot(hn, w_ref[:, sl]).astype(BF16)
    if with_small:
        os_ref[...] = _dot(hn, ws_ref[...])


def norm_matmul(x, gain, w, chunk, w_small=None, tm=1024):
    t, d = x.shape
    n = w.shape[1]
    with_small = w_small is not None
    in_specs = [pl.BlockSpec((tm, d), lambda i: (i, 0)),
                pl.BlockSpec((1, d), lambda i: (0, 0)),
                pl.BlockSpec((d, n), lambda i: (0, 0), pipeline_mode=pl.Buffered(1))]
    out_specs = [pl.BlockSpec((tm, n), lambda i: (i, 0))]
    out_shape = [jax.ShapeDtypeStruct((t, n), BF16)]
    args = [x, gain.reshape(1, d), w]
    if with_small:
        in_specs.append(pl.BlockSpec((d, LANES), lambda i: (0, 0)))
        out_specs.append(pl.BlockSpec((tm, LANES), lambda i: (i, 0)))
        out_shape.append(jax.ShapeDtypeStruct((t, LANES), F32))
        args.append(w_small)
    outs = pl.pallas_call(
        functools.partial(_norm_matmul_body, chunk=chunk, with_small=with_small),
        grid=(t // tm,), in_specs=in_specs, out_specs=out_specs, out_shape=out_shape,
        compiler_params=_cparams(1), name="norm_matmul")(*args)
    return outs if with_small else outs[0]


def _merge_body(x_ref, ydn_ref, yssm_ref, yattn_ref, gate_ref, wdn_ref, wssm_ref, wattn_ref, wout_ref, o_ref):
    d = x_ref.shape[1]

    def sig(j):
        return jax.nn.sigmoid(gate_ref[:, j * d:(j + 1) * d].astype(F32))

    m = sig(0) * _dot(ydn_ref[...], wdn_ref[...])
    m = m + sig(1) * _dot(yssm_ref[...], wssm_ref[...])
    m = m + sig(2) * _dot(yattn_ref[...], wattn_ref[...])
    o_ref[...] = x_ref[...] + _dot(m.astype(BF16), wout_ref[...])


def merge_out(x, y_dn, y_ssm, y_attn, gates, w_dn, w_ssm, w_attn, w_out, tm=512):
    t, d = x.shape

    def rows(width):
        return pl.BlockSpec((tm, width), lambda i: (i, 0))

    def whole(a):
        return pl.BlockSpec(a.shape, lambda i: (0, 0), pipeline_mode=pl.Buffered(1))

    return pl.pallas_call(
        _merge_body, grid=(t // tm,),
        in_specs=[rows(d), rows(y_dn.shape[1]), rows(y_ssm.shape[1]), rows(y_attn.shape[1]), rows(gates.shape[1]),
                  whole(w_dn), whole(w_ssm), whole(w_attn), whole(w_out)],
        out_specs=rows(d), out_shape=jax.ShapeDtypeStruct((t, d), F32),
        compiler_params=_cparams(1), name="merge_out")(x, y_dn, y_ssm, y_attn, gates, w_dn, w_ssm, w_attn, w_out)


def _top2_gates(logits):
    lane = lax.broadcasted_iota(jnp.int32, logits.shape, 1)
    lg = jnp.where(lane < N_EXPERTS, logits, NEG_BIG)
    m1 = jnp.max(lg, axis=1, keepdims=True)
    i1 = jnp.min(jnp.where(lg == m1, lane, LANES), axis=1, keepdims=True)
    lg2 = jnp.where(lane == i1, NEG_BIG, lg)
    m2 = jnp.max(lg2, axis=1, keepdims=True)
    i2 = jnp.min(jnp.where(lg2 == m2, lane, LANES), axis=1, keepdims=True)
    e2 = jnp.exp(m2 - m1)
    w1 = 1.0 / (1.0 + e2)
    return jnp.where(lane == i1, w1, 0.0) + jnp.where(lane == i2, e2 * w1, 0.0)


def _ffn_body(*refs, sub, routed, final):
    it = iter(refs)
    x_ref, g_ref = next(it), next(it)
    wr_ref = next(it) if routed else None
    wg_ref, wu_ref, wd_ref, p_ref, pg_ref, wpg_ref, wp_ref = (next(it) for _ in range(7))
    fg_ref = next(it) if final else None
    o_ref, hn_scr, acc_scr = next(it), next(it), next(it)
    gates_scr = next(it) if routed else None
    e, f = pl.program_id(1), pl.program_id(2)
    last = jnp.logical_and(e == pl.num_programs(1) - 1, f == pl.num_programs(2) - 1)

    @pl.when(jnp.logical_and(e == 0, f == 0))
    def _():
        hn = _rms(x_ref[...], g_ref[...])
        hn_scr[...] = hn.astype(BF16)
        acc_scr[...] = jnp.zeros_like(acc_scr)
        if routed:
            logits = jnp.dot(hn, wr_ref[...], precision=lax.Precision.HIGHEST, preferred_element_type=F32)
            gates_scr[...] = _top2_gates(logits)

    hn = hn_scr[...]
    tf = wg_ref.shape[2]
    y = None
    for c in range(tf // sub):
        sl = slice(c * sub, (c + 1) * sub)
        a = _dot(hn, wg_ref[0, :, sl])
        u = _dot(hn, wu_ref[0, :, sl])
        act = (a * jax.nn.sigmoid(a) * u).astype(BF16)
        yc = _dot(act, wd_ref[0, sl, :])
        y = yc if y is None else y + yc
    if routed:
        lane = lax.broadcasted_iota(jnp.int32, gates_scr.shape, 1)
        y = y * jnp.sum(jnp.where(lane == e, gates_scr[...], 0.0), axis=1, keepdims=True)
    acc_scr[...] += y

    @pl.when(last)
    def _():
        x1 = x_ref[...] + acc_scr[...]
        gate = jax.nn.sigmoid(_dot(_rms(x1, pg_ref[...]).astype(BF16), wpg_ref[...]))
        x2 = x1 + _dot(p_ref[...].astype(BF16), wp_ref[...]) * gate
        if final:
            x2 = _rms(x2, fg_ref[...])
        o_ref[...] = x2


def ffn_ple(x, ffn_gain, wg, wu, wd, p, ple_gain, w_ple_gate, w_ple, *, tm, tf, sub, w_router=None, final_gain=None):
    t, d = x.shape
    n_e, _, f_dim = wg.shape
    routed = w_router is not None
    final = final_gain is not None
    const = dict(pipeline_mode=pl.Buffered(1)) if n_e * (f_dim // tf) == 1 else {}
    in_specs = [pl.BlockSpec((tm, d), lambda i, e, f: (i, 0)), pl.BlockSpec((1, d), lambda i, e, f: (0, 0))]
    args = [x, ffn_gain.reshape(1, d)]
    if routed:
        in_specs.append(pl.BlockSpec((d, LANES), lambda i, e, f: (0, 0)))
        args.append(w_router)
    in_specs += [pl.BlockSpec((1, d, tf), lambda i, e, f: (e, 0, f), **const),
                 pl.BlockSpec((1, d, tf), lambda i, e, f: (e, 0, f), **const),
                 pl.BlockSpec((1, tf, d), lambda i, e, f: (e, f, 0), **const),
                 pl.BlockSpec((tm, p.shape[1]), lambda i, e, f: (i, 0)),
                 pl.BlockSpec((1, d), lambda i, e, f: (0, 0)),
                 pl.BlockSpec((d, d), lambda i, e, f: (0, 0), pipeline_mode=pl.Buffered(1)),
                 pl.BlockSpec((p.shape[1], d), lambda i, e, f: (0, 0), pipeline_mode=pl.Buffered(1))]
    args += [wg, wu, wd, p, ple_gain.reshape(1, d), w_ple_gate, w_ple]
    if final:
        in_specs.append(pl.BlockSpec((1, d), lambda i, e, f: (0, 0)))
        args.append(final_gain.reshape(1, d))
    scratch = [pltpu.VMEM((tm, d), BF16), pltpu.VMEM((tm, d), F32)]
    if routed:
        scratch.append(pltpu.VMEM((tm, LANES), F32))
    return pl.pallas_call(
        functools.partial(_ffn_body, sub=sub, routed=routed, final=final),
        grid=(t // tm, n_e, f_dim // tf), in_specs=in_specs,
        out_specs=pl.BlockSpec((tm, d), lambda i, e, f: (i, 0)), out_shape=jax.ShapeDtypeStruct((t, d), F32),
        scratch_shapes=scratch, compiler_params=_cparams(3), name="ffn_ple")(*args)


def _causal_conv(x, w):
    k_w = w.shape[0]
    s = x.shape[1]
    xp = jnp.pad(x, ((0, 0), (k_w - 1, 0), (0, 0)))
    y = xp[:, k_w - 1:k_w - 1 + s] * w[k_w - 1]
    for j in range(k_w - 1):
        y = y + xp[:, j:j + s] * w[j]
    return y


def _l2norm(x):
    return x * lax.rsqrt(jnp.sum(x * x, axis=-1, keepdims=True) + RMS_EPS)


def _rmsn(x, g):
    return x * lax.rsqrt(jnp.mean(x * x, axis=-1, keepdims=True) + RMS_EPS) * g


def _chunked_gated_delta_rule(q, k, v, g, beta):
    bsz, s, h, dk = q.shape
    dv = v.shape[-1]
    c = DN_CHUNK
    n = s // c

    def to_chunks(t):
        return jnp.moveaxis(t.reshape(bsz, n, c, h, *t.shape[3:]), 3, 2)

    q, k, v, g, beta = [to_chunks(t) for t in (q, k, v, g, beta)]
    q = q * dk ** -0.5
    gam = jnp.cumsum(g, axis=-1)
    incl = jnp.tril(jnp.ones((c, c), bool))
    strict = jnp.tril(jnp.ones((c, c), bool), k=-1)
    diff = gam[..., :, None] - gam[..., None, :]
    decay = jnp.where(incl, jnp.exp(jnp.where(incl, diff, 0.0)), 0.0)
    kb = k * beta[..., None]
    a_mat = jnp.where(strict, jnp.einsum('bnhid,bnhjd->bnhij', kb, k) * decay, 0.0)
    lower = a_mat + jnp.eye(c, dtype=F32)
    rhs = jnp.concatenate([kb * jnp.exp(gam)[..., None], v * beta[..., None]], axis=-1)
    sol = lax.linalg.triangular_solve(lower, rhs, left_side=True, lower=True, unit_diagonal=True)
    w_c, u_c = sol[..., :dk], sol[..., dk:]
    q_dec = q * jnp.exp(gam)[..., None]
    attn_qk = jnp.einsum('bnhid,bnhjd->bnhij', q, k) * decay
    g_last = gam[..., -1]
    k_dec = k * jnp.exp(g_last[..., None] - gam)[..., None]

    def step(state, xs):
        w_i, u_i, qd_i, aqk_i, kd_i, gl_i = xs
        v_new = u_i - jnp.einsum('bhcd,bhde->bhce', w_i, state)
        o_i = jnp.einsum('bhcd,bhde->bhce', qd_i, state) + jnp.einsum('bhij,bhje->bhie', aqk_i, v_new)
        state = state * jnp.exp(gl_i)[..., None, None] + jnp.einsum('bhcd,bhce->bhde', kd_i, v_new)
        return state, o_i

    xs = tuple(jnp.moveaxis(t, 1, 0) for t in (w_c, u_c, q_dec, attn_qk, k_dec, g_last))
    _, o = lax.scan(step, jnp.zeros((bsz, h, dk, dv), F32), xs)
    o = jnp.moveaxis(jnp.moveaxis(o, 0, 1), 3, 2)
    return o.reshape(bsz, s, h, dv)


def _gated_deltanet(qkv, z, a, b, conv_w, a_log, dt_bias, norm_g):
    bsz, s, _ = qkv.shape
    qkv = jax.nn.silu(_causal_conv(qkv, conv_w))
    q, k, v = jnp.split(qkv, 3, axis=-1)
    q = _l2norm(q.reshape(bsz, s, DN_HEADS, DN_HEAD_DIM))
    k = _l2norm(k.reshape(bsz, s, DN_HEADS, DN_HEAD_DIM))
    v = v.reshape(bsz, s, DN_HEADS, DN_HEAD_DIM)
    g = -jnp.exp(a_log) * jax.nn.softplus(a + dt_bias)
    beta = jax.nn.sigmoid(b)
    o = _chunked_gated_delta_rule(q, k, v, g, beta)
    o = _rmsn(o, norm_g) * jax.nn.silu(z.reshape(bsz, s, DN_HEADS, DN_HEAD_DIM))
    return o.reshape(bsz, s, DN_WIDTH)


def _ssd_chunked(x, dt, a_neg, bm, cm):
    bsz, s, h, pd = x.shape
    grp, ns = bm.shape[2], bm.shape[3]
    hpg = h // grp
    c = SSM_CHUNK
    nc = s // c
    xdt = (x * dt[..., None]).reshape(bsz, nc, c, grp, hpg, pd)
    a = jnp.moveaxis((dt * a_neg).reshape(bsz, nc, c, grp, hpg), 2, -1)
    bc = bm.reshape(bsz, nc, c, grp, ns)
    cc = cm.reshape(bsz, nc, c, grp, ns)
    acs = jnp.cumsum(a, axis=-1)
    incl = jnp.tril(jnp.ones((c, c), bool))
    diff = acs[..., :, None] - acs[..., None, :]
    lmat = jnp.where(incl, jnp.exp(jnp.where(incl, diff, 0.0)), 0.0)
    cb = jnp.einsum('bcigs,bcjgs->bcgij', cc, bc)
    y_diag = jnp.einsum('bcgij,bcghij,bcjghp->bcighp', cb, lmat, xdt)
    decay_end = jnp.exp(acs[..., -1:] - acs)
    states = jnp.einsum('bcjgs,bcghj,bcjghp->bcghps', bc, decay_end, xdt)
    chunk_decay = jnp.exp(acs[..., -1])

    def step(state, xs):
        st_i, dec_i = xs
        return state * dec_i[..., None, None] + st_i, state

    _, prev = lax.scan(step, jnp.zeros((bsz, grp, hpg, pd, ns), F32), (jnp.moveaxis(states, 1, 0), jnp.moveaxis(chunk_decay, 1, 0)))
    prev = jnp.moveaxis(prev, 0, 1)
    y_off = jnp.einsum('bcigs,bcghps,bcghi->bcighp', cc, prev, jnp.exp(acs))
    return (y_diag + y_off).reshape(bsz, s, h, pd)


def _mamba2_ssd(xbc, z, dt, conv_w, conv_b, a_log, dt_bias, d_skip, norm_g):
    bsz, s, _ = xbc.shape
    xbc = jax.nn.silu(_causal_conv(xbc, conv_w) + conv_b)
    xs, bm, cm = jnp.split(xbc, [SSM_WIDTH, SSM_WIDTH + SSM_GROUPS * SSM_STATE], axis=-1)
    x = xs.reshape(bsz, s, SSM_HEADS, SSM_HEAD_DIM)
    bm = bm.reshape(bsz, s, SSM_GROUPS, SSM_STATE)
    cm = cm.reshape(bsz, s, SSM_GROUPS, SSM_STATE)
    dt = jax.nn.softplus(dt + dt_bias)
    y = _ssd_chunked(x, dt, -jnp.exp(a_log), bm, cm) + d_skip[:, None] * x
    gw = SSM_WIDTH // SSM_GROUPS
    y = y.reshape(bsz, s, SSM_GROUPS, gw) * jax.nn.silu(z.reshape(bsz, s, SSM_GROUPS, gw))
    y = _rmsn(y, norm_g.reshape(SSM_GROUPS, gw))
    return y.reshape(bsz, s, SSM_WIDTH)


def _dilated_window_attention(q, k, v, slopes, window, dil):
    bsz, s, hg, e = q.shape
    sub_len = s // dil
    w_sub = window // dil
    c = ATTN_BLOCK
    nb = -(-sub_len // c)
    lp = nb * c

    def to_sub(t):
        return t.reshape(bsz, sub_len, dil, hg, e).transpose(0, 2, 3, 1, 4)

    qs, ks, vs = to_sub(q), to_sub(k), to_sub(v)
    qb = jnp.pad(qs, ((0, 0), (0, 0), (0, 0), (0, lp - sub_len), (0, 0))).reshape(bsz, dil, hg, nb, c, e)

    def band(t):
        tp = jnp.pad(t, ((0, 0), (0, 0), (0, 0), (c, lp - sub_len), (0, 0))).reshape(bsz, dil, hg, nb + 1, c, e)
        return jnp.concatenate([tp[:, :, :, :-1], tp[:, :, :, 1:]], axis=-2)

    kb, vb = band(ks), band(vs)
    scores = jnp.einsum('bdhnqe,bdhnke->bdhnqk', qb, kb) * e ** -0.5
    qpos = jnp.arange(c)[:, None]
    kpos = jnp.arange(2 * c)[None, :]
    delta = c + qpos - kpos
    key_idx = (jnp.arange(nb)[:, None, None] - 1) * c + kpos[None]
    valid = (delta >= 0) & (delta <= w_sub) & (key_idx >= 0)
    bias = -slopes[:, None, None] * (delta * dil).astype(F32)
    scores = jnp.where(valid, scores + bias[:, None], -jnp.inf)
    m = jnp.max(scores, axis=-1, keepdims=True)
    ex = jnp.exp(scores - m)
    den = jnp.sum(ex, axis=-1, keepdims=True)
    o = jnp.einsum('bdhnqk,bdhnke->bdhnqe', ex / den, vb)
    lse = (m + jnp.log(den))[..., 0]
    o = o.reshape(bsz, dil, hg, lp, e)[:, :, :, :sub_len].transpose(0, 3, 1, 2, 4).reshape(bsz, s, hg, e)
    lse = lse.reshape(bsz, dil, hg, lp)[..., :sub_len].transpose(0, 3, 1, 2).reshape(bsz, s, hg)
    return o, lse


def _dilated_attention(qkv):
    bsz, s, _ = qkv.shape
    qkv = qkv.reshape(bsz, s, 3, ATTN_HEADS, ATTN_HEAD_DIM)
    q, k, v = qkv[:, :, 0], qkv[:, :, 1], qkv[:, :, 2]
    slopes = jnp.exp2(-ALIBI_MAX_BIAS * (jnp.arange(ATTN_HEADS, dtype=F32) + 1.0) / ATTN_HEADS)
    outs, lses = [], []
    for gi, (window, dil) in enumerate(DILATION_GROUPS):
        hs = slice(gi * ATTN_GROUP_HEADS, (gi + 1) * ATTN_GROUP_HEADS)
        o, lse = _dilated_window_attention(q[:, :, hs], k[:, :, hs], v[:, :, hs], slopes[hs], window, dil)
        outs.append(o)
        lses.append(lse)
    wts = jax.nn.softmax(jnp.stack(lses, axis=0), axis=0)
    o = jnp.sum(wts[..., None] * jnp.stack(outs, axis=0), axis=0)
    return o.reshape(bsz, s, ATTN_OUT_WIDTH)


def _split_w_in(w_in):
    cuts = np.cumsum(IN_SPLIT_SIZES)[:-1].tolist()
    dn_qkv, dn_z, dn_a, dn_b, ssm_xbc, ssm_z, ssm_dt, attn_qkv, gate = jnp.split(w_in, cuts, axis=-1)
    w_dn = jnp.concatenate([dn_qkv, dn_z], axis=1).astype(BF16)
    w_ssm = jnp.concatenate([ssm_xbc[:, :SSM_WIDTH], ssm_z, ssm_xbc[:, SSM_WIDTH:]], axis=1).astype(BF16)
    small = jnp.concatenate([dn_a, dn_b, ssm_dt], axis=1)
    w_small = jnp.pad(small, ((0, 0), (0, LANES - small.shape[1]))).astype(BF16)
    return w_dn, w_ssm, attn_qkv.astype(BF16), gate.astype(BF16), w_small


def kernel(x, p, mix_norm, w_in, dn_conv, dn_a_log, dn_dt_bias, dn_norm, ssm_conv, ssm_conv_b, ssm_a_log, ssm_dt_bias, ssm_d, ssm_norm, w_br_dn, w_br_ssm, w_br_attn, w_out, ffn_norm, w_ff_gate, w_ff_up, w_ff_down, w_router, w_moe_gate, w_moe_up, w_moe_down, ple_norm, w_ple, w_ple_gate, final_norm):
    bsz, s, d = x.shape
    t = bsz * s
    depth = w_in.shape[0]
    x = x.reshape(t, d)
    for i in range(depth):
        w_dn, w_ssm, w_attn, w_gate, w_small = _split_w_in(w_in[i])
        g = mix_norm[i]
        pr_dn = norm_matmul(x, g, w_dn, 512)
        pr_ssm, small = norm_matmul(x, g, w_ssm, 512, w_small=w_small)
        pr_attn = norm_matmul(x, g, w_attn, 768)
        gates = norm_matmul(x, g, w_gate, 512)

        f = lambda a: a.astype(F32).reshape(bsz, s, -1)
        sm = small.reshape(bsz, s, LANES)
        dn = f(pr_dn)
        y_dn = _gated_deltanet(dn[..., :3 * DN_WIDTH], dn[..., 3 * DN_WIDTH:], sm[..., 0:6], sm[..., 6:12],
                               dn_conv[i], dn_a_log[i], dn_dt_bias[i], dn_norm[i])
        ss = f(pr_ssm)
        xbc = jnp.concatenate([ss[..., :SSM_WIDTH], ss[..., 2 * SSM_WIDTH:]], axis=-1)
        y_ssm = _mamba2_ssd(xbc, ss[..., SSM_WIDTH:2 * SSM_WIDTH], sm[..., 12:24], ssm_conv[i], ssm_conv_b[i],
                            ssm_a_log[i], ssm_dt_bias[i], ssm_d[i], ssm_norm[i])
        y_attn = _dilated_attention(f(pr_attn))

        x = merge_out(x, y_dn.reshape(t, -1).astype(BF16), y_ssm.reshape(t, -1).astype(BF16),
                      y_attn.reshape(t, -1).astype(BF16), gates,
                      w_br_dn[i].astype(BF16), w_br_ssm[i].astype(BF16), w_br_attn[i].astype(BF16), w_out[i].astype(BF16))

        j = i // 2
        fin = final_norm if i == depth - 1 else None
        common = (p[i].reshape(t, -1), ple_norm[i], w_ple_gate[i].astype(BF16), w_ple[i].astype(BF16))
        if i % 2 == 0:
            x = ffn_ple(x, ffn_norm[i], w_ff_gate[j][None].astype(BF16), w_ff_up[j][None].astype(BF16),
                        w_ff_down[j][None].astype(BF16), *common, tm=1024, tf=FFN_DIM, sub=256, final_gain=fin)
        else:
            wr = jnp.pad(w_router[j], ((0, 0), (0, LANES - N_EXPERTS)))
            x = ffn_ple(x, ffn_norm[i], w_moe_gate[j].astype(BF16), w_moe_up[j].astype(BF16), w_moe_down[j].astype(BF16),
                        *common, tm=512, tf=EXPERT_DIM // 2, sub=256, w_router=wr, final_gain=fin)
    return x.reshape(bsz, s, d)
```

```python
import functools

import numpy as np
import jax
import jax.numpy as jnp
from jax import lax
from jax.experimental import pallas as pl
from jax.experimental.pallas import tpu as pltpu

F32 = jnp.float32
BF16 = jnp.bfloat16

D_MODEL = 1024
PLE_DIM = 256
CONV_WIDTH = 4
RMS_EPS = 1e-6
DN_HEADS = 6
DN_HEAD_DIM = 128
DN_WIDTH = DN_HEADS * DN_HEAD_DIM
SSM_HEADS = 12
SSM_HEAD_DIM = 64
SSM_WIDTH = SSM_HEADS * SSM_HEAD_DIM
SSM_GROUPS = 2
SSM_STATE = 128
SSM_XBC = SSM_WIDTH + 2 * SSM_GROUPS * SSM_STATE
SSM_GW = SSM_WIDTH // SSM_GROUPS
SSM_HPG = SSM_HEADS // SSM_GROUPS
ATTN_HEADS = 12
ATTN_HEAD_DIM = 64
ATTN_QKV = ATTN_HEADS * ATTN_HEAD_DIM
DILATION_GROUPS = ((128, 1), (512, 4), (2048, 16))
N_DIL = len(DILATION_GROUPS)
ATTN_GROUP_HEADS = ATTN_HEADS // N_DIL
ATTN_OUT_WIDTH = ATTN_GROUP_HEADS * ATTN_HEAD_DIM
ATTN_BLOCK = 128
ALIBI_MAX_BIAS = 8.0
N_BRANCHES = 3
FFN_DIM = 2816
N_EXPERTS = 8
EXPERT_DIM = 3584
IN_SPLIT_SIZES = (3 * DN_WIDTH, DN_WIDTH, DN_HEADS, DN_HEADS, SSM_XBC, SSM_WIDTH, SSM_HEADS, 3 * ATTN_QKV, N_BRANCHES * D_MODEL)

LANES = 128
HALVES = ATTN_OUT_WIDTH // LANES
HEADS_PER_HALF = LANES // ATTN_HEAD_DIM
DT_LANE0 = 2 * DN_HEADS
CHUNK = 128
NEG_BIG = -1e30
VMEM_LIMIT_BYTES = 56 * 1024 * 1024


def _cparams(n_axes):
    return pltpu.CompilerParams(dimension_semantics=("arbitrary",) * n_axes, vmem_limit_bytes=VMEM_LIMIT_BYTES)


def _rms(x, g):
    return x * lax.rsqrt(jnp.mean(x * x, axis=-1, keepdims=True) + RMS_EPS) * g


def _dot(a, b):
    return jnp.dot(a, b, preferred_element_type=F32)


def _dot_nt(a, b):
    return lax.dot_general(a, b, (((1,), (1,)), ((), ())), preferred_element_type=F32)


def _dot_tn(a, b):
    return lax.dot_general(a, b, (((0,), (0,)), ((), ())), preferred_element_type=F32)


def _silu(x):
    return x * jax.nn.sigmoid(x)


def _softplus(x):
    return jnp.maximum(x, 0.0) + jnp.log1p(jnp.exp(-jnp.abs(x)))


def _norm_matmul_body(x_ref, g_ref, w_ref, *rest, chunk, with_small):
    if with_small:
        ws_ref, o_ref, os_ref = rest
    else:
        (o_ref,) = rest
    hn = _rms(x_ref[...], g_ref[...]).astype(BF16)
    for c in range(w_ref.shape[1] // chunk):
        sl = slice(c * chunk, (c + 1) * chunk)
        o_ref[:, sl] = _dot(hn, w_ref[:, sl]).astype(BF16)
    if with_small:
        os_ref[...] = _dot(hn, ws_ref[...])


def norm_matmul(x, gain, w, chunk, w_small=None, tm=1024):
    t, d = x.shape
    n = w.shape[1]
    with_small = w_small is not None
    in_specs = [pl.BlockSpec((tm, d), lambda i: (i, 0)),
                pl.BlockSpec((1, d), lambda i: (0, 0)),
                pl.BlockSpec((d, n), lambda i: (0, 0), pipeline_mode=pl.Buffered(1))]
    out_specs = [pl.BlockSpec((tm, n), lambda i: (i, 0))]
    out_shape = [jax.ShapeDtypeStruct((t, n), BF16)]
    args = [x, gain.reshape(1, d), w]
    if with_small:
        in_specs.append(pl.BlockSpec((d, LANES), lambda i: (0, 0)))
        out_specs.append(pl.BlockSpec((tm, LANES), lambda i: (i, 0)))
        out_shape.append(jax.ShapeDtypeStruct((t, LANES), F32))
        args.append(w_small)
    outs = pl.pallas_call(
        functools.partial(_norm_matmul_body, chunk=chunk, with_small=with_small),
        grid=(t // tm,), in_specs=in_specs, out_specs=out_specs, out_shape=out_shape,
        compiler_params=_cparams(1), name="norm_matmul")(*args)
    return outs if with_small else outs[0]


def _merge_body(x_ref, ydn_ref, yssm_ref, yattn_ref, gate_ref, wdn_ref, wssm_ref, wattn_ref, wout_ref, o_ref):
    d = x_ref.shape[1]

    def sig(j):
        return jax.nn.sigmoid(gate_ref[:, j * d:(j + 1) * d].astype(F32))

    m = sig(0) * _dot(ydn_ref[...], wdn_ref[...])
    m = m + sig(1) * _dot(yssm_ref[...], wssm_ref[...])
    m = m + sig(2) * _dot(yattn_ref[...], wattn_ref[...])
    o_ref[...] = x_ref[...] + _dot(m.astype(BF16), wout_ref[...])


def merge_out(x, y_dn, y_ssm, y_attn, gates, w_dn, w_ssm, w_attn, w_out, tm=512):
    t, d = x.shape

    def rows(width):
        return pl.BlockSpec((tm, width), lambda i: (i, 0))

    def whole(a):
        return pl.BlockSpec(a.shape, lambda i: (0, 0), pipeline_mode=pl.Buffered(1))

    return pl.pallas_call(
        _merge_body, grid=(t // tm,),
        in_specs=[rows(d), rows(y_dn.shape[1]), rows(y_ssm.shape[1]), rows(y_attn.shape[1]), rows(gates.shape[1]),
                  whole(w_dn), whole(w_ssm), whole(w_attn), whole(w_out)],
        out_specs=rows(d), out_shape=jax.ShapeDtypeStruct((t, d), F32),
        compiler_params=_cparams(1), name="merge_out")(x, y_dn, y_ssm, y_attn, gates, w_dn, w_ssm, w_attn, w_out)


def _top2_gates(logits):
    lane = lax.broadcasted_iota(jnp.int32, logits.shape, 1)
    lg = jnp.where(lane < N_EXPERTS, logits, NEG_BIG)
    m1 = jnp.max(lg, axis=1, keepdims=True)
    i1 = jnp.min(jnp.where(lg == m1, lane, LANES), axis=1, keepdims=True)
    lg2 = jnp.where(lane == i1, NEG_BIG, lg)
    m2 = jnp.max(lg2, axis=1, keepdims=True)
    i2 = jnp.min(jnp.where(lg2 == m2, lane, LANES), axis=1, keepdims=True)
    e2 = jnp.exp(m2 - m1)
    w1 = 1.0 / (1.0 + e2)
    return jnp.where(lane == i1, w1, 0.0) + jnp.where(lane == i2, e2 * w1, 0.0)


def _ffn_body(*refs, sub, routed, final):
    it = iter(refs)
    x_ref, g_ref = next(it), next(it)
    wr_ref = next(it) if routed else None
    wg_ref, wu_ref, wd_ref, p_ref, pg_ref, wpg_ref, wp_ref = (next(it) for _ in range(7))
    fg_ref = next(it) if final else None
    o_ref, hn_scr, acc_scr = next(it), next(it), next(it)
    gates_scr = next(it) if routed else None
    e, f = pl.program_id(1), pl.program_id(2)
    last = jnp.logical_and(e == pl.num_programs(1) - 1, f == pl.num_programs(2) - 1)

    @pl.when(jnp.logical_and(e == 0, f == 0))
    def _():
        hn = _rms(x_ref[...], g_ref[...])
        hn_scr[...] = hn.astype(BF16)
        acc_scr[...] = jnp.zeros_like(acc_scr)
        if routed:
            logits = jnp.dot(hn, wr_ref[...], precision=lax.Precision.HIGHEST, preferred_element_type=F32)
            gates_scr[...] = _top2_gates(logits)

    hn = hn_scr[...]
    tf = wg_ref.shape[2]
    y = None
    for c in range(tf // sub):
        sl = slice(c * sub, (c + 1) * sub)
        a = _dot(hn, wg_ref[0, :, sl])
        u = _dot(hn, wu_ref[0, :, sl])
        yc = _dot((_silu(a) * u).astype(BF16), wd_ref[0, sl, :])
        y = yc if y is None else y + yc
    if routed:
        lane = lax.broadcasted_iota(jnp.int32, gates_scr.shape, 1)
        y = y * jnp.sum(jnp.where(lane == e, gates_scr[...], 0.0), axis=1, keepdims=True)
    acc_scr[...] += y

    @pl.when(last)
    def _():
        x1 = x_ref[...] + acc_scr[...]
        gate = jax.nn.sigmoid(_dot(_rms(x1, pg_ref[...]).astype(BF16), wpg_ref[...]))
        x2 = x1 + _dot(p_ref[...].astype(BF16), wp_ref[...]) * gate
        if final:
            x2 = _rms(x2, fg_ref[...])
        o_ref[...] = x2


def ffn_ple(x, ffn_gain, wg, wu, wd, p, ple_gain, w_ple_gate, w_ple, *, tm, tf, sub, w_router=None, final_gain=None):
    t, d = x.shape
    n_e, _, f_dim = wg.shape
    routed = w_router is not None
    final = final_gain is not None
    const = dict(pipeline_mode=pl.Buffered(1)) if n_e * (f_dim // tf) == 1 else {}
    in_specs = [pl.BlockSpec((tm, d), lambda i, e, f: (i, 0)), pl.BlockSpec((1, d), lambda i, e, f: (0, 0))]
    args = [x, ffn_gain.reshape(1, d)]
    if routed:
        in_specs.append(pl.BlockSpec((d, LANES), lambda i, e, f: (0, 0)))
        args.append(w_router)
    in_specs += [pl.BlockSpec((1, d, tf), lambda i, e, f: (e, 0, f), **const),
                 pl.BlockSpec((1, d, tf), lambda i, e, f: (e, 0, f), **const),
                 pl.BlockSpec((1, tf, d), lambda i, e, f: (e, f, 0), **const),
                 pl.BlockSpec((tm, p.shape[1]), lambda i, e, f: (i, 0)),
                 pl.BlockSpec((1, d), lambda i, e, f: (0, 0)),
                 pl.BlockSpec((d, d), lambda i, e, f: (0, 0), pipeline_mode=pl.Buffered(1)),
                 pl.BlockSpec((p.shape[1], d), lambda i, e, f: (0, 0), pipeline_mode=pl.Buffered(1))]
    args += [wg, wu, wd, p, ple_gain.reshape(1, d), w_ple_gate, w_ple]
    if final:
        in_specs.append(pl.BlockSpec((1, d), lambda i, e, f: (0, 0)))
        args.append(final_gain.reshape(1, d))
    scratch = [pltpu.VMEM((tm, d), BF16), pltpu.VMEM((tm, d), F32)]
    if routed:
        scratch.append(pltpu.VMEM((tm, LANES), F32))
    return pl.pallas_call(
        functools.partial(_ffn_body, sub=sub, routed=routed, final=final),
        grid=(t // tm, n_e, f_dim // tf), in_specs=in_specs,
        out_specs=pl.BlockSpec((tm, d), lambda i, e, f: (i, 0)), out_shape=jax.ShapeDtypeStruct((t, d), F32),
        scratch_shapes=scratch, compiler_params=_cparams(3), name="ffn_ple")(*args)


def _causal_conv_silu(xf, w_ref, rows, bias=None):
    k_w = w_ref.shape[0]
    y = xf * w_ref[k_w - 1:k_w, :]
    for j in range(k_w - 1):
        sh = k_w - 1 - j
        y = y + jnp.where(rows >= sh, pltpu.roll(xf, sh, 0), 0.0) * w_ref[j:j + 1, :]
    if bias is not None:
        y = y + bias
    return _silu(y)


def _chunk_cumsum(x, rows):
    s = 1
    while s < CHUNK:
        x = x + jnp.where(rows % CHUNK >= s, pltpu.roll(x, s, 0), 0.0)
        s *= 2
    return x


def _tri_iotas():
    return lax.broadcasted_iota(jnp.int32, (CHUNK, CHUNK), 0), lax.broadcasted_iota(jnp.int32, (CHUNK, CHUNK), 1)


def _decay_matrix(gm_b, incl):
    diff = gm_b - gm_b.T
    return jnp.where(incl, jnp.exp(jnp.where(incl, diff, 0.0)), 0.0)


def _unit_lower_inverse(a, ri, ci):
    eye = (ri == ci).astype(F32)
    t = eye - jnp.where((ri - ci == 1) & (ri % 2 == 1), a, 0.0)
    s = 2
    while s < CHUNK:
        off = jnp.where(((ri // s) % 2 == 1) & (ci // s == ri // s - 1), a, 0.0)
        tb = t.astype(BF16)
        t = t - _dot(tb, _dot(off.astype(BF16), tb).astype(BF16))
        s *= 2
    return t


def _dn_body(alog_ref, dtb_ref, q_ref, k_ref, v_ref, z_ref, sm_ref, cwq_ref, cwk_ref, cwv_ref, ng_ref, o_ref,
             qn_scr, kn_scr, vc_scr, gam_scr, beta_scr):
    h = pl.program_id(1)
    s = q_ref.shape[0]
    rows = lax.broadcasted_iota(jnp.int32, (s, LANES), 0)
    lane = lax.broadcasted_iota(jnp.int32, (s, LANES), 1)

    def l2n(x):
        return x * lax.rsqrt(jnp.sum(x * x, axis=-1, keepdims=True) + RMS_EPS)

    qn_scr[...] = l2n(_causal_conv_silu(q_ref[...].astype(F32), cwq_ref, rows)) * DN_HEAD_DIM ** -0.5
    kn_scr[...] = l2n(_causal_conv_silu(k_ref[...].astype(F32), cwk_ref, rows))
    vc_scr[...] = _causal_conv_silu(v_ref[...].astype(F32), cwv_ref, rows)

    sm = sm_ref[...]
    a_col = jnp.sum(jnp.where(lane == h, sm, 0.0), axis=1, keepdims=True)
    b_col = jnp.sum(jnp.where(lane == DN_HEADS + h, sm, 0.0), axis=1, keepdims=True)
    a_neg = -jnp.exp(jnp.zeros((1, LANES), F32) + alog_ref[h])
    g = a_neg * _softplus(jnp.broadcast_to(a_col, (s, LANES)) + dtb_ref[h])
    gam_scr[...] = _chunk_cumsum(g, rows)
    beta_scr[...] = jnp.broadcast_to(jax.nn.sigmoid(b_col), (s, LANES))

    ri, ci = _tri_iotas()
    incl = ri >= ci
    strict = ri > ci

    def chunk(c, state):
        sl = pl.ds(pl.multiple_of(c * CHUNK, CHUNK), CHUNK)
        qc, kc, vc, gm, bt = qn_scr[sl, :], kn_scr[sl, :], vc_scr[sl, :], gam_scr[sl, :], beta_scr[sl, :]
        decay = _decay_matrix(gm, incl)
        kb = kc * bt
        kcb = kc.astype(BF16)
        a_mat = jnp.where(strict, _dot_nt(kb.astype(BF16), kcb) * decay, 0.0)
        t_inv = _unit_lower_inverse(a_mat, ri, ci).astype(BF16)
        eg = jnp.exp(gm)
        w_c = _dot(t_inv, (kb * eg).astype(BF16))
        u_c = _dot(t_inv, (vc * bt).astype(BF16))
        aqk = _dot_nt(qc.astype(BF16), kcb) * decay
        sb = state.astype(BF16)
        v_new = u_c - _dot(w_c.astype(BF16), sb)
        vnb = v_new.astype(BF16)
        o = _dot((qc * eg).astype(BF16), sb) + _dot(aqk.astype(BF16), vnb)
        g_last = gm[CHUNK - 1:CHUNK, :]
        state = state * jnp.exp(g_last) + _dot_tn((kc * jnp.exp(g_last - gm)).astype(BF16), vnb)
        zc = z_ref[sl, :].astype(F32)
        on = o * lax.rsqrt(jnp.mean(o * o, axis=-1, keepdims=True) + RMS_EPS) * ng_ref[...]
        o_ref[sl, :] = (on * _silu(zc)).astype(BF16)
        return state

    lax.fori_loop(0, s // CHUNK, chunk, jnp.zeros((DN_HEAD_DIM, DN_HEAD_DIM), F32))


def gated_deltanet(pr_dn, small, conv_w, a_log, dt_bias, norm_g, bsz, s):
    t = bsz * s
    hd = DN_HEAD_DIM

    def col(j0):
        return pl.BlockSpec((s, hd), lambda b, h, j0=j0: (b, j0 + h))

    def cw(j0):
        return pl.BlockSpec((CONV_WIDTH, hd), lambda b, h, j0=j0: (0, j0 + h))

    smem = pl.BlockSpec(memory_space=pltpu.SMEM)
    return pl.pallas_call(
        _dn_body, grid=(bsz, DN_HEADS),
        in_specs=[smem, smem, col(0), col(DN_HEADS), col(2 * DN_HEADS), col(3 * DN_HEADS),
                  pl.BlockSpec((s, LANES), lambda b, h: (b, 0)),
                  cw(0), cw(DN_HEADS), cw(2 * DN_HEADS), pl.BlockSpec((1, hd), lambda b, h: (0, 0))],
        out_specs=pl.BlockSpec((s, hd), lambda b, h: (b, h)),
        out_shape=jax.ShapeDtypeStruct((t, DN_WIDTH), BF16),
        scratch_shapes=[pltpu.VMEM((s, hd), F32)] * 5,
        compiler_params=_cparams(2), name="gated_deltanet")(
            a_log, dt_bias, pr_dn, pr_dn, pr_dn, pr_dn, small, conv_w, conv_w, conv_w, norm_g.reshape(1, hd))


def _expand_heads(v, g, n_rows):
    lane_head = lax.broadcasted_iota(jnp.int32, (n_rows, SSM_GW), 1) // SSM_HEAD_DIM
    out = jnp.zeros((n_rows, SSM_GW), F32)
    for hh in range(SSM_HPG):
        l0 = DT_LANE0 + g * SSM_HPG + hh
        out = jnp.where(lane_head == hh, v[:, l0:l0 + 1], out)
    return out


def _ssd_body(pr_ref, sm_ref, cw_ref, cb_ref, alog_ref, dtb_ref, d_ref, ng_ref, o_ref,
              x_scr, b_scr, c_scr, dt_scr, acs_scr, st_scr):
    s = pr_ref.shape[0]
    rows = lax.broadcasted_iota(jnp.int32, (s, LANES), 0)
    n_state = SSM_GROUPS * SSM_STATE
    for j in range(SSM_XBC // LANES):
        cs = slice(j * LANES, (j + 1) * LANES)
        src = j * LANES if j * LANES < SSM_WIDTH else j * LANES + SSM_WIDTH
        y = _causal_conv_silu(pr_ref[:, src:src + LANES].astype(F32), cw_ref.at[:, cs], rows, bias=cb_ref[:, cs])
        if j * LANES < SSM_WIDTH:
            x_scr[:, cs] = y
        elif j * LANES < SSM_WIDTH + n_state:
            b_scr[:, j * LANES - SSM_WIDTH:(j + 1) * LANES - SSM_WIDTH] = y
        else:
            c_scr[:, j * LANES - SSM_WIDTH - n_state:(j + 1) * LANES - SSM_WIDTH - n_state] = y
    dt = _softplus(sm_ref[...] + dtb_ref[...])
    dt_scr[...] = dt
    acs_scr[...] = _chunk_cumsum(dt * (-jnp.exp(alog_ref[...])), rows)
    st_scr[...] = jnp.zeros_like(st_scr)

    ri, ci = _tri_iotas()
    incl = ri >= ci

    def chunk(c, carry):
        sl = pl.ds(pl.multiple_of(c * CHUNK, CHUNK), CHUNK)
        dt_c, acs_c = dt_scr[sl, :], acs_scr[sl, :]
        acs_last = acs_c[CHUNK - 1:CHUNK, :]
        for g in range(SSM_GROUPS):
            gs = slice(g * SSM_GW, (g + 1) * SSM_GW)
            bc = b_scr[sl, g * SSM_STATE:(g + 1) * SSM_STATE].astype(BF16)
            cc = c_scr[sl, g * SSM_STATE:(g + 1) * SSM_STATE].astype(BF16)
            xg = x_scr[sl, gs]
            xdt = xg * _expand_heads(dt_c, g, CHUNK)
            acs_e = _expand_heads(acs_c, g, CHUNK)
            last_e = _expand_heads(acs_last, g, 1)
            cbm = _dot_nt(cc, bc)
            state = st_scr[g]
            y = _dot(cc, state.astype(BF16)) * jnp.exp(acs_e) + _expand_heads(d_ref[...], g, 1) * xg
            st_scr[g] = state * jnp.exp(last_e) + _dot_tn(bc, (xdt * jnp.exp(last_e - acs_e)).astype(BF16))
            y_diag = []
            for hh in range(SSM_HPG):
                l0 = DT_LANE0 + g * SSM_HPG + hh
                lm = _decay_matrix(jnp.broadcast_to(acs_c[:, l0:l0 + 1], (CHUNK, CHUNK)), incl)
                y_diag.append(_dot((cbm * lm).astype(BF16), xdt[:, hh * SSM_HEAD_DIM:(hh + 1) * SSM_HEAD_DIM].astype(BF16)))
            y = y + jnp.concatenate(y_diag, axis=1)
            y = y * _silu(pr_ref[sl, SSM_WIDTH + g * SSM_GW:SSM_WIDTH + (g + 1) * SSM_GW].astype(F32))
            y = y * lax.rsqrt(jnp.mean(y * y, axis=-1, keepdims=True) + RMS_EPS) * ng_ref[:, gs]
            o_ref[sl, gs] = y.astype(BF16)
        return carry

    lax.fori_loop(0, s // CHUNK, chunk, 0)


def mamba2_ssd(pr_ssm, small, conv_w, conv_b, a_log, dt_bias, d_skip, norm_g, bsz, s):
    t = bsz * s

    def lanes(v):
        return jnp.zeros((1, LANES), F32).at[0, DT_LANE0:DT_LANE0 + SSM_HEADS].set(v)

    def whole(shape):
        return pl.BlockSpec(shape, lambda b: (0, 0))

    n_state = SSM_GROUPS * SSM_STATE
    return pl.pallas_call(
        _ssd_body, grid=(bsz,),
        in_specs=[pl.BlockSpec((s, pr_ssm.shape[1]), lambda b: (b, 0)), pl.BlockSpec((s, LANES), lambda b: (b, 0)),
                  whole((CONV_WIDTH, SSM_XBC)), whole((1, SSM_XBC)), whole((1, LANES)), whole((1, LANES)), whole((1, LANES)),
                  whole((1, SSM_WIDTH))],
        out_specs=pl.BlockSpec((s, SSM_WIDTH), lambda b: (b, 0)),
        out_shape=jax.ShapeDtypeStruct((t, SSM_WIDTH), BF16),
        scratch_shapes=[pltpu.VMEM((s, SSM_WIDTH), F32), pltpu.VMEM((s, n_state), F32), pltpu.VMEM((s, n_state), F32),
                        pltpu.VMEM((s, LANES), F32), pltpu.VMEM((s, LANES), F32),
                        pltpu.VMEM((SSM_GROUPS, SSM_STATE, SSM_GW), F32)],
        compiler_params=_cparams(1), name="mamba2_ssd")(
            pr_ssm, small, conv_w, conv_b.reshape(1, -1), lanes(a_log), lanes(dt_bias), lanes(d_skip), norm_g.reshape(1, -1))


def _attn_body(pr_ref, o_ref, qf, kf, vf, og, lg):
    s = pr_ref.shape[0]
    c = ATTN_BLOCK
    e = ATTN_HEAD_DIM
    ii = lax.broadcasted_iota(jnp.int32, (c, c), 0)
    jj = lax.broadcasted_iota(jnp.int32, (c, c), 1)
    for gi, (window, dil) in enumerate(DILATION_GROUPS):
        w_sub = window // dil
        nb = s // dil // c
        for j in range(HALVES):
            c0 = gi * ATTN_OUT_WIDTH + j * LANES
            qf[j] = pr_ref[:, c0:c0 + LANES].astype(F32) * e ** -0.5
            kf[j] = pr_ref[:, ATTN_QKV + c0:ATTN_QKV + c0 + LANES].astype(F32)
            vf[j] = pr_ref[:, 2 * ATTN_QKV + c0:2 * ATTN_QKV + c0 + LANES].astype(F32)
        d_cur = ii - jj
        d_prev = d_cur + c
        ok_cur = (d_cur >= 0) & (d_cur <= w_sub)
        ok_prev_static = d_prev <= w_sub

        def block(idx, carry, gi=gi, dil=dil, nb=nb, d_cur=d_cur, d_prev=d_prev, ok_cur=ok_cur, ok_prev_static=ok_prev_static):
            r, n = idx // nb, idx % nb
            cur = pl.ds(r + n * (c * dil), c, stride=dil)
            prev = pl.ds(r + jnp.maximum(n - 1, 0) * (c * dil), c, stride=dil)
            ok_prev = ok_prev_static & (n > 0)
            for j in range(HALVES):
                qb = qf.at[j][cur, :].astype(BF16)
                kc, kp = kf.at[j][cur, :].astype(BF16), kf.at[j][prev, :].astype(BF16)
                vc, vp = vf.at[j][cur, :].astype(BF16), vf.at[j][prev, :].astype(BF16)
                outs, lses = [], []
                for hh in range(HEADS_PER_HALF):
                    head = gi * ATTN_GROUP_HEADS + j * HEADS_PER_HALF + hh
                    slope = float(2.0 ** (-ALIBI_MAX_BIAS * (head + 1.0) / ATTN_HEADS)) * dil
                    hs = slice(hh * e, (hh + 1) * e)
                    sc = jnp.where(ok_cur, _dot_nt(qb[:, hs], kc[:, hs]) - slope * d_cur.astype(F32), NEG_BIG)
                    sp = jnp.where(ok_prev, _dot_nt(qb[:, hs], kp[:, hs]) - slope * d_prev.astype(F32), NEG_BIG)
                    m = jnp.maximum(jnp.max(sc, axis=1, keepdims=True), jnp.max(sp, axis=1, keepdims=True))
                    pc, pp = jnp.exp(sc - m), jnp.exp(sp - m)
                    den = jnp.sum(pc, axis=1, keepdims=True) + jnp.sum(pp, axis=1, keepdims=True)
                    outs.append((_dot(pc.astype(BF16), vc[:, hs]) + _dot(pp.astype(BF16), vp[:, hs])) / den)
                    lses.append(jnp.broadcast_to(m + jnp.log(den), (c, e)))
                og.at[gi, j][cur, :] = jnp.concatenate(outs, axis=1)
                lg.at[gi, j][cur, :] = jnp.concatenate(lses, axis=1)
            return carry

        lax.fori_loop(0, dil * nb, block, 0)
    for j in range(HALVES):
        m = jnp.maximum(jnp.maximum(lg[0, j], lg[1, j]), lg[2, j])
        num = jnp.zeros((s, LANES), F32)
        den = jnp.zeros((s, LANES), F32)
        for gi in range(N_DIL):
            w = jnp.exp(lg[gi, j] - m)
            num = num + w * og[gi, j]
            den = den + w
        o_ref[:, j * LANES:(j + 1) * LANES] = (num / den).astype(BF16)


def dilated_attention(pr_attn, bsz, s):
    t = bsz * s
    assert N_DIL == 3
    for window, dil in DILATION_GROUPS:
        assert s % (dil * ATTN_BLOCK) == 0 and window // dil <= ATTN_BLOCK
    return pl.pallas_call(
        _attn_body, grid=(bsz,),
        in_specs=[pl.BlockSpec((s, pr_attn.shape[1]), lambda b: (b, 0))],
        out_specs=pl.BlockSpec((s, ATTN_OUT_WIDTH), lambda b: (b, 0)),
        out_shape=jax.ShapeDtypeStruct((t, ATTN_OUT_WIDTH), BF16),
        scratch_shapes=[pltpu.VMEM((HALVES, s, LANES), F32)] * 3 + [pltpu.VMEM((N_DIL, HALVES, s, LANES), F32)] * 2,
        compiler_params=_cparams(1), name="dilated_attention")(pr_attn)


def _split_w_in(w_in):
    cuts = np.cumsum(IN_SPLIT_SIZES)[:-1].tolist()
    dn_qkv, dn_z, dn_a, dn_b, ssm_xbc, ssm_z, ssm_dt, attn_qkv, gate = jnp.split(w_in, cuts, axis=-1)
    w_dn = jnp.concatenate([dn_qkv, dn_z], axis=1).astype(BF16)
    w_ssm = jnp.concatenate([ssm_xbc[:, :SSM_WIDTH], ssm_z, ssm_xbc[:, SSM_WIDTH:]], axis=1).astype(BF16)
    small = jnp.concatenate([dn_a, dn_b, ssm_dt], axis=1)
    w_small = jnp.pad(small, ((0, 0), (0, LANES - small.shape[1]))).astype(BF16)
    return w_dn, w_ssm, attn_qkv.astype(BF16), gate.astype(BF16), w_small


def kernel(x, p, mix_norm, w_in, dn_conv, dn_a_log, dn_dt_bias, dn_norm, ssm_conv, ssm_conv_b, ssm_a_log, ssm_dt_bias, ssm_d, ssm_norm, w_br_dn, w_br_ssm, w_br_attn, w_out, ffn_norm, w_ff_gate, w_ff_up, w_ff_down, w_router, w_moe_gate, w_moe_up, w_moe_down, ple_norm, w_ple, w_ple_gate, final_norm):
    bsz, s, d = x.shape
    t = bsz * s
    depth = w_in.shape[0]
    x = x.reshape(t, d)
    for i in range(depth):
        w_dn, w_ssm, w_attn, w_gate, w_small = _split_w_in(w_in[i])
        g = mix_norm[i]
        pr_dn = norm_matmul(x, g, w_dn, 512)
        pr_ssm, small = norm_matmul(x, g, w_ssm, 512, w_small=w_small)
        pr_attn = norm_matmul(x, g, w_attn, 768)
        gates = norm_matmul(x, g, w_gate, 512)

        y_dn = gated_deltanet(pr_dn, small, dn_conv[i], dn_a_log[i], dn_dt_bias[i], dn_norm[i], bsz, s)
        y_ssm = mamba2_ssd(pr_ssm, small, ssm_conv[i], ssm_conv_b[i], ssm_a_log[i], ssm_dt_bias[i], ssm_d[i], ssm_norm[i], bsz, s)
        y_attn = dilated_attention(pr_attn, bsz, s)

        x = merge_out(x, y_dn, y_ssm, y_attn, gates,
                      w_br_dn[i].astype(BF16), w_br_ssm[i].astype(BF16), w_br_attn[i].astype(BF16), w_out[i].astype(BF16))

        j = i // 2
        fin = final_norm if i == depth - 1 else None
        common = (p[i].reshape(t, -1), ple_norm[i], w_ple_gate[i].astype(BF16), w_ple[i].astype(BF16))
        if i % 2 == 0:
            x = ffn_ple(x, ffn_norm[i], w_ff_gate[j][None].astype(BF16), w_ff_up[j][None].astype(BF16),
                        w_ff_down[j][None].astype(BF16), *common, tm=1024, tf=FFN_DIM, sub=256, final_gain=fin)
        else:
            wr = jnp.pad(w_router[j], ((0, 0), (0, LANES - N_EXPERTS)))
            x = ffn_ple(x, ffn_norm[i], w_moe_gate[j].astype(BF16), w_moe_up[j].astype(BF16), w_moe_down[j].astype(BF16),
                        *common, tm=512, tf=EXPERT_DIM // 2, sub=256, w_router=wr, final_gain=fin)
    return x.reshape(bsz, s, d)
```

```python
import functools

import numpy as np
import jax
import jax.numpy as jnp
from jax import lax
from jax.experimental import pallas as pl
from jax.experimental.pallas import tpu as pltpu

F32 = jnp.float32
BF16 = jnp.bfloat16

D_MODEL = 1024
PLE_DIM = 256
CONV_WIDTH = 4
RMS_EPS = 1e-6
DN_HEADS = 6
DN_HEAD_DIM = 128
DN_WIDTH = DN_HEADS * DN_HEAD_DIM
SSM_HEADS = 12
SSM_HEAD_DIM = 64
SSM_WIDTH = SSM_HEADS * SSM_HEAD_DIM
SSM_GROUPS = 2
SSM_STATE = 128
SSM_XBC = SSM_WIDTH + 2 * SSM_GROUPS * SSM_STATE
SSM_GW = SSM_WIDTH // SSM_GROUPS
SSM_HPG = SSM_HEADS // SSM_GROUPS
ATTN_HEADS = 12
ATTN_HEAD_DIM = 64
ATTN_QKV = ATTN_HEADS * ATTN_HEAD_DIM
DILATION_GROUPS = ((128, 1), (512, 4), (2048, 16))
N_DIL = len(DILATION_GROUPS)
ATTN_GROUP_HEADS = ATTN_HEADS // N_DIL
ATTN_OUT_WIDTH = ATTN_GROUP_HEADS * ATTN_HEAD_DIM
ATTN_BLOCK = 128
ALIBI_MAX_BIAS = 8.0
N_BRANCHES = 3
FFN_DIM = 2816
N_EXPERTS = 8
EXPERT_DIM = 3584
IN_SPLIT_SIZES = (3 * DN_WIDTH, DN_WIDTH, DN_HEADS, DN_HEADS, SSM_XBC, SSM_WIDTH, SSM_HEADS, 3 * ATTN_QKV, N_BRANCHES * D_MODEL)

LANES = 128
HALVES = ATTN_OUT_WIDTH // LANES
HEADS_PER_HALF = LANES // ATTN_HEAD_DIM
DT_LANE0 = 2 * DN_HEADS
CHUNK = 128
DN_HEADS_PER_STEP = 2
DN_UNROLL = 8
DN_STEP_W = DN_HEADS_PER_STEP * DN_HEAD_DIM
NEG_BIG = -1e30
VMEM_LIMIT_BYTES = 56 * 1024 * 1024


def _cparams(n_axes):
    return pltpu.CompilerParams(dimension_semantics=("arbitrary",) * n_axes, vmem_limit_bytes=VMEM_LIMIT_BYTES)


def _rms(x, g):
    return x * lax.rsqrt(jnp.mean(x * x, axis=-1, keepdims=True) + RMS_EPS) * g


def _dot(a, b):
    return jnp.dot(a, b, preferred_element_type=F32)


def _dot_nt(a, b):
    return lax.dot_general(a, b, (((1,), (1,)), ((), ())), preferred_element_type=F32)


def _dot_tn(a, b):
    return lax.dot_general(a, b, (((0,), (0,)), ((), ())), preferred_element_type=F32)


def _silu(x):
    return x * jax.nn.sigmoid(x)


def _softplus(x):
    return jnp.maximum(x, 0.0) + jnp.log1p(jnp.exp(-jnp.abs(x)))


def _norm_matmul_body(x_ref, g_ref, w_ref, *rest, chunk, with_small):
    if with_small:
        ws_ref, o_ref, os_ref = rest
    else:
        (o_ref,) = rest
    hn = _rms(x_ref[...], g_ref[...]).astype(BF16)
    for c in range(w_ref.shape[1] // chunk):
        sl = slice(c * chunk, (c + 1) * chunk)
        o_ref[:, sl] = _dot(hn, w_ref[:, sl]).astype(BF16)
    if with_small:
        os_ref[...] = _dot(hn, ws_ref[...])


def norm_matmul(x, gain, w, chunk, w_small=None, tm=1024):
    t, d = x.shape
    n = w.shape[1]
    with_small = w_small is not None
    in_specs = [pl.BlockSpec((tm, d), lambda i: (i, 0)),
                pl.BlockSpec((1, d), lambda i: (0, 0)),
                pl.BlockSpec((d, n), lambda i: (0, 0), pipeline_mode=pl.Buffered(1))]
    out_specs = [pl.BlockSpec((tm, n), lambda i: (i, 0))]
    out_shape = [jax.ShapeDtypeStruct((t, n), BF16)]
    args = [x, gain.reshape(1, d), w]
    if with_small:
        in_specs.append(pl.BlockSpec((d, LANES), lambda i: (0, 0)))
        out_specs.append(pl.BlockSpec((tm, LANES), lambda i: (i, 0)))
        out_shape.append(jax.ShapeDtypeStruct((t, LANES), F32))
        args.append(w_small)
    outs = pl.pallas_call(
        functools.partial(_norm_matmul_body, chunk=chunk, with_small=with_small),
        grid=(t // tm,), in_specs=in_specs, out_specs=out_specs, out_shape=out_shape,
        compiler_params=_cparams(1), name="norm_matmul")(*args)
    return outs if with_small else outs[0]


def _merge_body(x_ref, ydn_ref, yssm_ref, yattn_ref, gate_ref, wdn_ref, wssm_ref, wattn_ref, wout_ref, o_ref):
    d = x_ref.shape[1]

    def sig(j):
        return jax.nn.sigmoid(gate_ref[:, j * d:(j + 1) * d].astype(F32))

    m = sig(0) * _dot(ydn_ref[...], wdn_ref[...])
    m = m + sig(1) * _dot(yssm_ref[...], wssm_ref[...])
    m = m + sig(2) * _dot(yattn_ref[...], wattn_ref[...])
    o_ref[...] = x_ref[...] + _dot(m.astype(BF16), wout_ref[...])


def merge_out(x, y_dn, y_ssm, y_attn, gates, w_dn, w_ssm, w_attn, w_out, tm=512):
    t, d = x.shape

    def rows(width):
        return pl.BlockSpec((tm, width), lambda i: (i, 0))

    def whole(a):
        return pl.BlockSpec(a.shape, lambda i: (0, 0), pipeline_mode=pl.Buffered(1))

    return pl.pallas_call(
        _merge_body, grid=(t // tm,),
        in_specs=[rows(d), rows(y_dn.shape[1]), rows(y_ssm.shape[1]), rows(y_attn.shape[1]), rows(gates.shape[1]),
                  whole(w_dn), whole(w_ssm), whole(w_attn), whole(w_out)],
        out_specs=rows(d), out_shape=jax.ShapeDtypeStruct((t, d), F32),
        compiler_params=_cparams(1), name="merge_out")(x, y_dn, y_ssm, y_attn, gates, w_dn, w_ssm, w_attn, w_out)


def _swiglu(hn, wg_ref, wu_ref, wd_ref, sub):
    y = None
    for c in range(wg_ref.shape[2] // sub):
        sl = slice(c * sub, (c + 1) * sub)
        a = _dot(hn, wg_ref[0, :, sl])
        u = _dot(hn, wu_ref[0, :, sl])
        yc = _dot((_silu(a) * u).astype(BF16), wd_ref[0, sl, :])
        y = yc if y is None else y + yc
    return y


def _ple_epilogue(x1, p_ref, pg_ref, wpg_ref, wp_ref, fg_ref):
    gate = jax.nn.sigmoid(_dot(_rms(x1, pg_ref[...]).astype(BF16), wpg_ref[...]))
    x2 = x1 + _dot(p_ref[...].astype(BF16), wp_ref[...]) * gate
    return x2 if fg_ref is None else _rms(x2, fg_ref[...])


def _ffn_body(*refs, sub, final):
    it = iter(refs)
    x_ref, g_ref, wg_ref, wu_ref, wd_ref, p_ref, pg_ref, wpg_ref, wp_ref = (next(it) for _ in range(9))
    fg_ref = next(it) if final else None
    o_ref = next(it)
    x = x_ref[...]
    y = _swiglu(_rms(x, g_ref[...]).astype(BF16), wg_ref, wu_ref, wd_ref, sub)
    o_ref[...] = _ple_epilogue(x + y, p_ref, pg_ref, wpg_ref, wp_ref, fg_ref)


def ffn_ple(x, ffn_gain, wg, wu, wd, p, ple_gain, w_ple_gate, w_ple, *, tm, sub, final_gain=None):
    t, d = x.shape
    final = final_gain is not None

    def whole(a):
        return pl.BlockSpec(a.shape, lambda i: (0,) * a.ndim, pipeline_mode=pl.Buffered(1))

    in_specs = [pl.BlockSpec((tm, d), lambda i: (i, 0)), pl.BlockSpec((1, d), lambda i: (0, 0)),
                whole(wg), whole(wu), whole(wd),
                pl.BlockSpec((tm, p.shape[1]), lambda i: (i, 0)), pl.BlockSpec((1, d), lambda i: (0, 0)),
                whole(w_ple_gate), whole(w_ple)]
    args = [x, ffn_gain.reshape(1, d), wg, wu, wd, p, ple_gain.reshape(1, d), w_ple_gate, w_ple]
    if final:
        in_specs.append(pl.BlockSpec((1, d), lambda i: (0, 0)))
        args.append(final_gain.reshape(1, d))
    return pl.pallas_call(
        functools.partial(_ffn_body, sub=sub, final=final), grid=(t // tm,), in_specs=in_specs,
        out_specs=pl.BlockSpec((tm, d), lambda i: (i, 0)), out_shape=jax.ShapeDtypeStruct((t, d), F32),
        compiler_params=_cparams(1), name="ffn_ple")(*args)


ROUTE_I1, ROUTE_I2, ROUTE_W1, ROUTE_W2 = 8, 9, 10, 11
MOE_TILE = 512
MOE_ROWS_STEP = 512


def _route_body(x_ref, g_ref, wr_ref, o_ref):
    hn = _rms(x_ref[...], g_ref[...])
    logits = jnp.dot(hn, wr_ref[...], precision=lax.Precision.HIGHEST, preferred_element_type=F32)
    lane = lax.broadcasted_iota(jnp.int32, logits.shape, 1)
    lg = jnp.where(lane < N_EXPERTS, logits, NEG_BIG)
    m1 = jnp.max(lg, axis=1, keepdims=True)
    i1 = jnp.min(jnp.where(lg == m1, lane, LANES), axis=1, keepdims=True)
    lg2 = jnp.where(lane == i1, NEG_BIG, lg)
    m2 = jnp.max(lg2, axis=1, keepdims=True)
    i2 = jnp.min(jnp.where(lg2 == m2, lane, LANES), axis=1, keepdims=True)
    e2 = jnp.exp(m2 - m1)
    w1 = 1.0 / (1.0 + e2)
    rec = jnp.where(lane == ROUTE_I1, i1.astype(F32), 0.0) + jnp.where(lane == ROUTE_I2, i2.astype(F32), 0.0)
    o_ref[...] = rec + jnp.where(lane == ROUTE_W1, w1, 0.0) + jnp.where(lane == ROUTE_W2, e2 * w1, 0.0)


def moe_route(x, gain, w_router, tm=1024):
    t, d = x.shape
    return pl.pallas_call(
        _route_body, grid=(t // tm,),
        in_specs=[pl.BlockSpec((tm, d), lambda i: (i, 0)), pl.BlockSpec((1, d), lambda i: (0, 0)),
                  pl.BlockSpec((d, LANES), lambda i: (0, 0))],
        out_specs=pl.BlockSpec((tm, LANES), lambda i: (i, 0)), out_shape=jax.ShapeDtypeStruct((t, LANES), F32),
        compiler_params=_cparams(1), name="moe_route")(x, gain.reshape(1, d), w_router)


def _row_positions(route, n_rows_pad):
    t = route.shape[0]
    ids = route[:, ROUTE_I1:ROUTE_I2 + 1].astype(jnp.int32).reshape(-1)
    onehot = (ids[:, None] == jnp.arange(N_EXPERTS, dtype=jnp.int32)[None, :]).astype(jnp.int32)
    csum = jnp.cumsum(onehot, axis=0)
    rank = jnp.sum((csum - onehot) * onehot, axis=1)
    counts = csum[-1]
    sizes = (counts + MOE_TILE - 1) // MOE_TILE * MOE_TILE
    ends = jnp.cumsum(sizes)
    pos = (ends - sizes)[ids] + rank
    tile_start = jnp.arange(n_rows_pad // MOE_TILE, dtype=jnp.int32) * MOE_TILE
    tile_valid = (tile_start < ends[-1]).astype(jnp.int32)
    tile_expert = jnp.minimum(jnp.sum((tile_start[:, None] >= ends[None, :]).astype(jnp.int32), axis=1), N_EXPERTS - 1)
    last_expert = jnp.max(jnp.where(sizes > 0, jnp.arange(N_EXPERTS, dtype=jnp.int32), 0))
    tile_expert = jnp.where(tile_valid == 1, tile_expert, last_expert)
    return pos.reshape(t // MOE_ROWS_STEP, 1, 2 * MOE_ROWS_STEP), tile_expert, tile_valid


def _scatter_rows_body(pos_ref, x_ref, xs_init_ref, xs_ref, sem):
    del xs_init_ref

    def row_copy(j, k):
        return pltpu.make_async_copy(x_ref.at[pl.ds(j, 1), :], xs_ref.at[pl.ds(pos_ref[0, 0, 2 * j + k], 1), :], sem)

    def start(j, c):
        row_copy(j, 0).start()
        row_copy(j, 1).start()
        return c

    def wait(j, c):
        row_copy(j, 0).wait()
        row_copy(j, 1).wait()
        return c

    lax.fori_loop(0, x_ref.shape[0], start, 0)
    lax.fori_loop(0, x_ref.shape[0], wait, 0)


def scatter_rows(x, pos, n_rows_pad):
    t, d = x.shape
    tm = MOE_ROWS_STEP
    return pl.pallas_call(
        _scatter_rows_body, grid=(t // tm,),
        in_specs=[pl.BlockSpec((1, 1, 2 * tm), lambda i: (i, 0, 0), memory_space=pltpu.SMEM),
                  pl.BlockSpec((tm, d), lambda i: (i, 0)), pl.BlockSpec(memory_space=pl.ANY)],
        out_specs=pl.BlockSpec(memory_space=pl.ANY), out_shape=jax.ShapeDtypeStruct((n_rows_pad, d), F32),
        scratch_shapes=[pltpu.SemaphoreType.DMA(())], input_output_aliases={2: 0},
        compiler_params=_cparams(1), name="moe_scatter_rows")(pos, x, jnp.zeros((n_rows_pad, d), F32))


def _expert_body(te_ref, tv_ref, xs_ref, g_ref, wg_ref, wu_ref, wd_ref, o_ref, *, sub):
    del te_ref

    @pl.when(tv_ref[pl.program_id(0)] == 1)
    def _():
        o_ref[...] = _swiglu(_rms(xs_ref[...], g_ref[...]).astype(BF16), wg_ref, wu_ref, wd_ref, sub)

    @pl.when(tv_ref[pl.program_id(0)] == 0)
    def _():
        o_ref[...] = jnp.zeros_like(o_ref)


def grouped_swiglu(xs, gain, wg, wu, wd, tile_expert, tile_valid, sub=512):
    n_rows, d = xs.shape
    f_dim = wg.shape[2]
    grid_spec = pltpu.PrefetchScalarGridSpec(
        num_scalar_prefetch=2, grid=(n_rows // MOE_TILE,),
        in_specs=[pl.BlockSpec((MOE_TILE, d), lambda i, te, tv: (i, 0)),
                  pl.BlockSpec((1, d), lambda i, te, tv: (0, 0)),
                  pl.BlockSpec((1, d, f_dim), lambda i, te, tv: (te[i], 0, 0), pipeline_mode=pl.Buffered(1)),
                  pl.BlockSpec((1, d, f_dim), lambda i, te, tv: (te[i], 0, 0), pipeline_mode=pl.Buffered(1)),
                  pl.BlockSpec((1, f_dim, d), lambda i, te, tv: (te[i], 0, 0), pipeline_mode=pl.Buffered(1))],
        out_specs=pl.BlockSpec((MOE_TILE, d), lambda i, te, tv: (i, 0)))
    return pl.pallas_call(
        functools.partial(_expert_body, sub=sub), grid_spec=grid_spec,
        out_shape=jax.ShapeDtypeStruct((n_rows, d), F32),
        compiler_params=_cparams(1), name="moe_grouped_swiglu")(tile_expert, tile_valid, xs, gain.reshape(1, d), wg, wu, wd)


def _combine_body(*refs, final):
    it = iter(refs)
    pos_ref, x_ref, route_ref, p_ref, pg_ref, wpg_ref, wp_ref = (next(it) for _ in range(7))
    fg_ref = next(it) if final else None
    ys_ref, o_ref, buf, sem = (next(it) for _ in range(4))

    def row_copy(j, k):
        return pltpu.make_async_copy(ys_ref.at[pl.ds(pos_ref[0, 0, 2 * j + k], 1), :], buf.at[k, pl.ds(j, 1), :], sem)

    def start(j, c):
        row_copy(j, 0).start()
        row_copy(j, 1).start()
        return c

    def wait(j, c):
        row_copy(j, 0).wait()
        row_copy(j, 1).wait()
        return c

    lax.fori_loop(0, x_ref.shape[0], start, 0)
    lax.fori_loop(0, x_ref.shape[0], wait, 0)
    route = route_ref[...]
    moe = route[:, ROUTE_W1:ROUTE_W1 + 1] * buf[0] + route[:, ROUTE_W2:ROUTE_W2 + 1] * buf[1]
    o_ref[...] = _ple_epilogue(x_ref[...] + moe, p_ref, pg_ref, wpg_ref, wp_ref, fg_ref)


def combine_ple(x, route, pos, ys, p, ple_gain, w_ple_gate, w_ple, final_gain=None):
    t, d = x.shape
    tm = MOE_ROWS_STEP
    final = final_gain is not None
    in_specs = [pl.BlockSpec((1, 1, 2 * tm), lambda i: (i, 0, 0), memory_space=pltpu.SMEM),
                pl.BlockSpec((tm, d), lambda i: (i, 0)), pl.BlockSpec((tm, LANES), lambda i: (i, 0)),
                pl.BlockSpec((tm, p.shape[1]), lambda i: (i, 0)), pl.BlockSpec((1, d), lambda i: (0, 0)),
                pl.BlockSpec((d, d), lambda i: (0, 0), pipeline_mode=pl.Buffered(1)),
                pl.BlockSpec((p.shape[1], d), lambda i: (0, 0), pipeline_mode=pl.Buffered(1))]
    args = [pos, x, route, p, ple_gain.reshape(1, d), w_ple_gate, w_ple]
    if final:
        in_specs.append(pl.BlockSpec((1, d), lambda i: (0, 0)))
        args.append(final_gain.reshape(1, d))
    in_specs.append(pl.BlockSpec(memory_space=pl.ANY))
    args.append(ys)
    return pl.pallas_call(
        functools.partial(_combine_body, final=final), grid=(t // tm,), in_specs=in_specs,
        out_specs=pl.BlockSpec((tm, d), lambda i: (i, 0)), out_shape=jax.ShapeDtypeStruct((t, d), F32),
        scratch_shapes=[pltpu.VMEM((2, tm, d), F32), pltpu.SemaphoreType.DMA(())],
        compiler_params=_cparams(1), name="moe_combine_ple")(*args)


def moe_ple(x, ffn_gain, w_router, wg, wu, wd, p, ple_gain, w_ple_gate, w_ple, final_gain=None):
    t = x.shape[0]
    n_rows_pad = 2 * t + N_EXPERTS * MOE_TILE
    route = moe_route(x, ffn_gain, w_router)
    pos, tile_expert, tile_valid = _row_positions(route, n_rows_pad)
    xs = scatter_rows(x, pos, n_rows_pad)
    ys = grouped_swiglu(xs, ffn_gain, wg, wu, wd, tile_expert, tile_valid)
    return combine_ple(x, route, pos, ys, p, ple_gain, w_ple_gate, w_ple, final_gain)


CONV_PAD = 8


def _causal_conv_silu(xf, w_ref, pad_ref, bias=None):
    k_w = w_ref.shape[0]
    s = xf.shape[0]
    pad_ref[0:CONV_PAD, :] = jnp.zeros((CONV_PAD, xf.shape[1]), F32)
    pad_ref[CONV_PAD:CONV_PAD + s, :] = xf
    y = xf * w_ref[k_w - 1:k_w, :]
    for j in range(k_w - 1):
        start = CONV_PAD - (k_w - 1 - j)
        y = y + pad_ref[start:start + s, :] * w_ref[j:j + 1, :]
    if bias is not None:
        y = y + bias
    return _silu(y)


def _chunk_cumsum(x, rows):
    s = 1
    while s < CHUNK:
        x = x + jnp.where(rows % CHUNK >= s, pltpu.roll(x, s, 0), 0.0)
        s *= 2
    return x


def _tri_iotas():
    return lax.broadcasted_iota(jnp.int32, (CHUNK, CHUNK), 0), lax.broadcasted_iota(jnp.int32, (CHUNK, CHUNK), 1)


def _decay_from(g_col, g_row, incl):
    return jnp.where(incl, jnp.exp(jnp.where(incl, g_col - g_row, 0.0)), 0.0)


def _decay_matrix(gm_b, incl):
    return _decay_from(gm_b, gm_b.T, incl)


def _unit_lower_inverses(mats, ri, ci):
    eye = (ri == ci).astype(F32)
    first = (ri - ci == 1) & (ri % 2 == 1)
    ts = [eye - jnp.where(first, a, 0.0) for a in mats]
    s = 2
    while s < CHUNK:
        off_mask = ((ri // s) % 2 == 1) & (ci // s == ri // s - 1)
        tbs = [t.astype(BF16) for t in ts]
        inner = [_dot(jnp.where(off_mask, a, 0.0).astype(BF16), tb).astype(BF16) for a, tb in zip(mats, tbs)]
        outer = [_dot(tb, x) for tb, x in zip(tbs, inner)]
        ts = [t - x for t, x in zip(ts, outer)]
        s *= 2
    return ts


def _dn_body(alog_ref, dtb_ref, q_ref, k_ref, v_ref, z_ref, sm_ref, cwq_ref, cwk_ref, cwv_ref, ng_ref, o_ref,
             qn_scr, kn_scr, vc_scr, gam_scr, beta_scr, p_scr, n_scr, qp_scr, o0_scr, gl_scr, pad_scr, gall_scr, ball_scr):
    s = q_ref.shape[0]
    hd = DN_HEAD_DIM
    n_chunks = s // CHUNK
    lane = lax.broadcasted_iota(jnp.int32, (s, LANES), 1)
    ri, ci = _tri_iotas()
    incl = ri >= ci
    strict = ri > ci

    def l2n(x):
        return x * lax.rsqrt(jnp.sum(x * x, axis=-1, keepdims=True) + RMS_EPS)

    @pl.when(pl.program_id(1) == 0)
    def _():
        rows = lax.broadcasted_iota(jnp.int32, (s, LANES), 0)
        sm = sm_ref[...]
        gall_scr[...] = _chunk_cumsum(-jnp.exp(alog_ref[...]) * _softplus(sm + dtb_ref[...]), rows)
        ball_scr[...] = jax.nn.sigmoid(sm)

    for hh in range(DN_HEADS_PER_STEP):
        h = pl.program_id(1) * DN_HEADS_PER_STEP + hh
        hs = slice(hh * hd, (hh + 1) * hd)
        qn_scr[...] = l2n(_causal_conv_silu(q_ref[:, hs].astype(F32), cwq_ref.at[:, hs], pad_scr)) * hd ** -0.5
        kn_scr[...] = l2n(_causal_conv_silu(k_ref[:, hs].astype(F32), cwk_ref.at[:, hs], pad_scr))
        vc_scr[...] = _causal_conv_silu(v_ref[:, hs].astype(F32), cwv_ref.at[:, hs], pad_scr)
        gam_scr[...] = jnp.broadcast_to(jnp.sum(jnp.where(lane == h, gall_scr[...], 0.0), axis=1, keepdims=True), (s, LANES))
        beta_scr[...] = jnp.broadcast_to(
            jnp.sum(jnp.where(lane == DN_HEADS + h, ball_scr[...], 0.0), axis=1, keepdims=True), (s, LANES))

        def factor_group(cg, carry, hh=hh):
            us = range(DN_UNROLL)
            sls = [pl.ds(pl.multiple_of((cg * DN_UNROLL + u) * CHUNK, CHUNK), CHUNK) for u in us]
            kc = [kn_scr[sl, :] for sl in sls]
            gm = [gam_scr[sl, :] for sl in sls]
            bt = [beta_scr[sl, :] for sl in sls]
            decay = [_decay_matrix(gm[u], incl) for u in us]
            kb = [kc[u] * bt[u] for u in us]
            kcb = [kc[u].astype(BF16) for u in us]
            kk = [_dot_nt(kb[u].astype(BF16), kcb[u]) for u in us]
            qk = [_dot_nt(qn_scr[sls[u], :].astype(BF16), kcb[u]) for u in us]
            t_inv = _unit_lower_inverses([jnp.where(strict, kk[u] * decay[u], 0.0) for u in us], ri, ci)
            eg = [jnp.exp(gm[u]) for u in us]
            rhs = [jnp.concatenate([kb[u] * eg[u], vc_scr[sls[u], :] * bt[u]], axis=1).astype(BF16) for u in us]
            wu = [_dot(t_inv[u].astype(BF16), rhs[u]).astype(BF16) for u in us]
            g_last = [gm[u][CHUNK - 1:CHUNK, :] for u in us]
            pn = [_dot_tn((kc[u] * jnp.exp(g_last[u] - gm[u])).astype(BF16), wu[u]) for u in us]
            qo = [_dot((qk[u] * decay[u]).astype(BF16), wu[u]) for u in us]
            for u in us:
                p_scr[hh, sls[u], :] = pn[u][:, :hd].astype(BF16)
                n_scr[hh, sls[u], :] = pn[u][:, hd:]
                qp_scr[hh, sls[u], :] = (qn_scr[sls[u], :] * eg[u] - qo[u][:, :hd]).astype(BF16)
                o0_scr[hh, sls[u], :] = qo[u][:, hd:]
                gl_scr[hh, pl.ds(pl.multiple_of((cg * DN_UNROLL + u) * 8, 8), 8), :] = jnp.broadcast_to(
                    jnp.exp(g_last[u]), (8, LANES))
            return carry

        lax.fori_loop(0, n_chunks // DN_UNROLL, factor_group, 0)

    def scan(c, states):
        sl = pl.ds(pl.multiple_of(c * CHUNK, CHUNK), CHUNK)
        new = []
        for hh in range(DN_HEADS_PER_STEP):
            hs = slice(hh * hd, (hh + 1) * hd)
            state = states[hh]
            sb = state.astype(BF16)
            o = _dot(qp_scr[hh, sl, :], sb) + o0_scr[hh, sl, :]
            e_last = gl_scr[hh, pl.ds(pl.multiple_of(c * 8, 8), 1), :]
            new.append(state * e_last - _dot(p_scr[hh, sl, :], sb) + n_scr[hh, sl, :])
            on = o * lax.rsqrt(jnp.mean(o * o, axis=-1, keepdims=True) + RMS_EPS) * ng_ref[...]
            o_ref[sl, hs] = (on * _silu(z_ref[sl, hs].astype(F32))).astype(BF16)
        return tuple(new)

    lax.fori_loop(0, n_chunks, scan, tuple(jnp.zeros((hd, hd), F32) for _ in range(DN_HEADS_PER_STEP)))


def gated_deltanet(pr_dn, small, conv_w, a_log, dt_bias, norm_g, bsz, s):
    t = bsz * s
    hd = DN_HEAD_DIM
    n_steps = DN_HEADS // DN_HEADS_PER_STEP
    assert (s // CHUNK) % DN_UNROLL == 0

    def col(j0):
        return pl.BlockSpec((s, DN_STEP_W), lambda b, h, j0=j0: (b, j0 + h))

    def cw(j0):
        return pl.BlockSpec((CONV_WIDTH, DN_STEP_W), lambda b, h, j0=j0: (0, j0 + h))

    def per_head(dtype):
        return pltpu.VMEM((DN_HEADS_PER_STEP, s, hd), dtype)

    def lanes(v):
        return jnp.zeros((1, LANES), F32).at[0, :DN_HEADS].set(v)

    lane_vec = pl.BlockSpec((1, LANES), lambda b, h: (0, 0))
    return pl.pallas_call(
        _dn_body, grid=(bsz, n_steps),
        in_specs=[lane_vec, lane_vec, col(0), col(n_steps), col(2 * n_steps), col(3 * n_steps),
                  pl.BlockSpec((s, LANES), lambda b, h: (b, 0)),
                  cw(0), cw(n_steps), cw(2 * n_steps), pl.BlockSpec((1, hd), lambda b, h: (0, 0))],
        out_specs=pl.BlockSpec((s, DN_STEP_W), lambda b, h: (b, h)),
        out_shape=jax.ShapeDtypeStruct((t, DN_WIDTH), BF16),
        scratch_shapes=[pltpu.VMEM((s, hd), F32)] * 5 + [per_head(BF16), per_head(F32), per_head(BF16), per_head(F32),
                                                        pltpu.VMEM((DN_HEADS_PER_STEP, s // CHUNK * 8, LANES), F32),
                                                        pltpu.VMEM((CONV_PAD + s, hd), F32),
                                                        pltpu.VMEM((s, LANES), F32), pltpu.VMEM((s, LANES), F32)],
        compiler_params=_cparams(2), name="gated_deltanet")(
            lanes(a_log), lanes(dt_bias), pr_dn, pr_dn, pr_dn, pr_dn, small, conv_w, conv_w, conv_w, norm_g.reshape(1, hd))


def _expand_heads(v, g, n_rows):
    lane_head = lax.broadcasted_iota(jnp.int32, (n_rows, SSM_GW), 1) // SSM_HEAD_DIM
    out = jnp.zeros((n_rows, SSM_GW), F32)
    for hh in range(SSM_HPG):
        l0 = DT_LANE0 + g * SSM_HPG + hh
        out = jnp.where(lane_head == hh, v[:, l0:l0 + 1], out)
    return out


def _ssd_body(pr_ref, sm_ref, cw_ref, cb_ref, alog_ref, dtb_ref, d_ref, ng_ref, o_ref,
              x_scr, b_scr, c_scr, dt_scr, acs_scr, st_scr, pad_scr):
    s = pr_ref.shape[0]
    rows = lax.broadcasted_iota(jnp.int32, (s, LANES), 0)
    n_state = SSM_GROUPS * SSM_STATE
    for j in range(SSM_XBC // LANES):
        cs = slice(j * LANES, (j + 1) * LANES)
        src = j * LANES if j * LANES < SSM_WIDTH else j * LANES + SSM_WIDTH
        y = _causal_conv_silu(pr_ref[:, src:src + LANES].astype(F32), cw_ref.at[:, cs], pad_scr, bias=cb_ref[:, cs])
        if j * LANES < SSM_WIDTH:
            x_scr[:, cs] = y
        elif j * LANES < SSM_WIDTH + n_state:
            b_scr[:, j * LANES - SSM_WIDTH:(j + 1) * LANES - SSM_WIDTH] = y.astype(BF16)
        else:
            c_scr[:, j * LANES - SSM_WIDTH - n_state:(j + 1) * LANES - SSM_WIDTH - n_state] = y.astype(BF16)
    dt = _softplus(sm_ref[...] + dtb_ref[...])
    dt_scr[...] = dt
    acs_scr[...] = _chunk_cumsum(dt * (-jnp.exp(alog_ref[...])), rows)
    st_scr[...] = jnp.zeros_like(st_scr)

    ri, ci = _tri_iotas()
    incl = ri >= ci

    def chunk(c, carry):
        sl = pl.ds(pl.multiple_of(c * CHUNK, CHUNK), CHUNK)
        dt_c, acs_c = dt_scr[sl, :], acs_scr[sl, :]
        acs_last = acs_c[CHUNK - 1:CHUNK, :]
        acs_t = acs_c.T
        groups = range(SSM_GROUPS)
        bc = [b_scr[sl, g * SSM_STATE:(g + 1) * SSM_STATE] for g in groups]
        cc = [c_scr[sl, g * SSM_STATE:(g + 1) * SSM_STATE] for g in groups]
        xg = [x_scr[sl, g * SSM_GW:(g + 1) * SSM_GW] for g in groups]
        xdt = [xg[g] * _expand_heads(dt_c, g, CHUNK) for g in groups]
        acs_e = [_expand_heads(acs_c, g, CHUNK) for g in groups]
        last_e = [_expand_heads(acs_last, g, 1) for g in groups]
        cbm = [_dot_nt(cc[g], bc[g]) for g in groups]
        state = [st_scr[g] for g in groups]
        y_off = [_dot(cc[g], state[g].astype(BF16)) for g in groups]
        st_new = [_dot_tn(bc[g], (xdt[g] * jnp.exp(last_e[g] - acs_e[g])).astype(BF16)) for g in groups]
        y_diag = []
        for g in groups:
            xdt_b = xdt[g].astype(BF16)
            for hh in range(SSM_HPG):
                l0 = DT_LANE0 + g * SSM_HPG + hh
                lm = _decay_from(jnp.broadcast_to(acs_c[:, l0:l0 + 1], (CHUNK, CHUNK)),
                                 jnp.broadcast_to(acs_t[l0:l0 + 1, :], (CHUNK, CHUNK)), incl)
                y_diag.append(_dot((cbm[g] * lm).astype(BF16), xdt_b[:, hh * SSM_HEAD_DIM:(hh + 1) * SSM_HEAD_DIM]))
        for g in groups:
            gs = slice(g * SSM_GW, (g + 1) * SSM_GW)
            st_scr[g] = state[g] * jnp.exp(last_e[g]) + st_new[g]
            y = y_off[g] * jnp.exp(acs_e[g]) + _expand_heads(d_ref[...], g, 1) * xg[g]
            y = y + jnp.concatenate(y_diag[g * SSM_HPG:(g + 1) * SSM_HPG], axis=1)
            y = y * _silu(pr_ref[sl, SSM_WIDTH + g * SSM_GW:SSM_WIDTH + (g + 1) * SSM_GW].astype(F32))
            y = y * lax.rsqrt(jnp.mean(y * y, axis=-1, keepdims=True) + RMS_EPS) * ng_ref[:, gs]
            o_ref[sl, gs] = y.astype(BF16)
        return carry

    lax.fori_loop(0, s // CHUNK, chunk, 0)


def mamba2_ssd(pr_ssm, small, conv_w, conv_b, a_log, dt_bias, d_skip, norm_g, bsz, s):
    t = bsz * s

    def lanes(v):
        return jnp.zeros((1, LANES), F32).at[0, DT_LANE0:DT_LANE0 + SSM_HEADS].set(v)

    def whole(shape):
        return pl.BlockSpec(shape, lambda b: (0, 0))

    n_state = SSM_GROUPS * SSM_STATE
    return pl.pallas_call(
        _ssd_body, grid=(bsz,),
        in_specs=[pl.BlockSpec((s, pr_ssm.shape[1]), lambda b: (b, 0), pipeline_mode=pl.Buffered(1)),
                  pl.BlockSpec((s, LANES), lambda b: (b, 0)),
                  whole((CONV_WIDTH, SSM_XBC)), whole((1, SSM_XBC)), whole((1, LANES)), whole((1, LANES)), whole((1, LANES)),
                  whole((1, SSM_WIDTH))],
        out_specs=pl.BlockSpec((s, SSM_WIDTH), lambda b: (b, 0)),
        out_shape=jax.ShapeDtypeStruct((t, SSM_WIDTH), BF16),
        scratch_shapes=[pltpu.VMEM((s, SSM_WIDTH), F32), pltpu.VMEM((s, n_state), BF16), pltpu.VMEM((s, n_state), BF16),
                        pltpu.VMEM((s, LANES), F32), pltpu.VMEM((s, LANES), F32),
                        pltpu.VMEM((SSM_GROUPS, SSM_STATE, SSM_GW), F32), pltpu.VMEM((CONV_PAD + s, LANES), F32)],
        compiler_params=_cparams(1), name="mamba2_ssd")(
            pr_ssm, small, conv_w, conv_b.reshape(1, -1), lanes(a_log), lanes(dt_bias), lanes(d_skip), norm_g.reshape(1, -1))


def _attn_body(pr_ref, o_ref, qf, kf, vf, og, lg):
    s = pr_ref.shape[0]
    c = ATTN_BLOCK
    e = ATTN_HEAD_DIM
    ii = lax.broadcasted_iota(jnp.int32, (c, c), 0)
    jj = lax.broadcasted_iota(jnp.int32, (c, c), 1)
    for gi, (window, dil) in enumerate(DILATION_GROUPS):
        w_sub = window // dil
        nb = s // dil // c
        for j in range(HALVES):
            c0 = gi * ATTN_OUT_WIDTH + j * LANES
            qf[j] = pr_ref[:, c0:c0 + LANES].astype(F32) * e ** -0.5
            kf[j] = pr_ref[:, ATTN_QKV + c0:ATTN_QKV + c0 + LANES].astype(F32)
            vf[j] = pr_ref[:, 2 * ATTN_QKV + c0:2 * ATTN_QKV + c0 + LANES].astype(F32)
        d_cur = ii - jj
        d_prev = d_cur + c
        ok_cur = (d_cur >= 0) & (d_cur <= w_sub)
        ok_prev_static = d_prev <= w_sub

        def block(idx, carry, gi=gi, dil=dil, nb=nb, d_cur=d_cur, d_prev=d_prev, ok_cur=ok_cur, ok_prev_static=ok_prev_static):
            r, n = idx // nb, idx % nb
            cur = pl.ds(r + n * (c * dil), c, stride=dil)
            prev = pl.ds(r + jnp.maximum(n - 1, 0) * (c * dil), c, stride=dil)
            ok_prev = ok_prev_static & (n > 0)
            heads = [(j, hh) for j in range(HALVES) for hh in range(HEADS_PER_HALF)]
            vcur = [vf.at[j][cur, :].astype(BF16) for j in range(HALVES)]
            vprev = [vf.at[j][prev, :].astype(BF16) for j in range(HALVES)]
            s_cur, s_prev = [], []
            for j in range(HALVES):
                qb = qf.at[j][cur, :].astype(BF16)
                kc, kp = kf.at[j][cur, :].astype(BF16), kf.at[j][prev, :].astype(BF16)
                for hh in range(HEADS_PER_HALF):
                    hs = slice(hh * e, (hh + 1) * e)
                    s_cur.append(_dot_nt(qb[:, hs], kc[:, hs]))
                    s_prev.append(_dot_nt(qb[:, hs], kp[:, hs]))
            probs, dens, lses = [], [], []
            for i, (j, hh) in enumerate(heads):
                head = gi * ATTN_GROUP_HEADS + j * HEADS_PER_HALF + hh
                slope = float(2.0 ** (-ALIBI_MAX_BIAS * (head + 1.0) / ATTN_HEADS)) * dil
                sc = jnp.where(ok_cur, s_cur[i] - slope * d_cur.astype(F32), NEG_BIG)
                sp = jnp.where(ok_prev, s_prev[i] - slope * d_prev.astype(F32), NEG_BIG)
                m = jnp.maximum(jnp.max(sc, axis=1, keepdims=True), jnp.max(sp, axis=1, keepdims=True))
                pc, pp = jnp.exp(sc - m), jnp.exp(sp - m)
                den = jnp.sum(pc, axis=1, keepdims=True) + jnp.sum(pp, axis=1, keepdims=True)
                probs.append((pc.astype(BF16), pp.astype(BF16)))
                dens.append(den)
                lses.append(jnp.broadcast_to(m + jnp.log(den), (c, e)))
            outs = []
            for i, (j, hh) in enumerate(heads):
                hs = slice(hh * e, (hh + 1) * e)
                outs.append((_dot(probs[i][0], vcur[j][:, hs]) + _dot(probs[i][1], vprev[j][:, hs])) / dens[i])
            for j in range(HALVES):
                hsl = slice(j * HEADS_PER_HALF, (j + 1) * HEADS_PER_HALF)
                og.at[gi, j][cur, :] = jnp.concatenate(outs[hsl], axis=1)
                lg.at[gi, j][cur, :] = jnp.concatenate(lses[hsl], axis=1)
            return carry

        lax.fori_loop(0, dil * nb, block, 0)
    for j in range(HALVES):
        m = jnp.maximum(jnp.maximum(lg[0, j], lg[1, j]), lg[2, j])
        num = jnp.zeros((s, LANES), F32)
        den = jnp.zeros((s, LANES), F32)
        for gi in range(N_DIL):
            w = jnp.exp(lg[gi, j] - m)
            num = num + w * og[gi, j]
            den = den + w
        o_ref[:, j * LANES:(j + 1) * LANES] = (num / den).astype(BF16)


def dilated_attention(pr_attn, bsz, s):
    t = bsz * s
    assert N_DIL == 3
    for window, dil in DILATION_GROUPS:
        assert s % (dil * ATTN_BLOCK) == 0 and window // dil <= ATTN_BLOCK
    return pl.pallas_call(
        _attn_body, grid=(bsz,),
        in_specs=[pl.BlockSpec((s, pr_attn.shape[1]), lambda b: (b, 0))],
        out_specs=pl.BlockSpec((s, ATTN_OUT_WIDTH), lambda b: (b, 0)),
        out_shape=jax.ShapeDtypeStruct((t, ATTN_OUT_WIDTH), BF16),
        scratch_shapes=[pltpu.VMEM((HALVES, s, LANES), F32)] * 3 + [pltpu.VMEM((N_DIL, HALVES, s, LANES), F32)] * 2,
        compiler_params=_cparams(1), name="dilated_attention")(pr_attn)


def _split_w_in(w_in):
    cuts = np.cumsum(IN_SPLIT_SIZES)[:-1].tolist()
    dn_qkv, dn_z, dn_a, dn_b, ssm_xbc, ssm_z, ssm_dt, attn_qkv, gate = jnp.split(w_in, cuts, axis=-1)
    w_dn = jnp.concatenate([dn_qkv, dn_z], axis=1).astype(BF16)
    w_ssm = jnp.concatenate([ssm_xbc[:, :SSM_WIDTH], ssm_z, ssm_xbc[:, SSM_WIDTH:]], axis=1).astype(BF16)
    small = jnp.concatenate([dn_a, dn_b, ssm_dt], axis=1)
    w_small = jnp.pad(small, ((0, 0), (0, LANES - small.shape[1]))).astype(BF16)
    return w_dn, w_ssm, attn_qkv.astype(BF16), gate.astype(BF16), w_small


def kernel(x, p, mix_norm, w_in, dn_conv, dn_a_log, dn_dt_bias, dn_norm, ssm_conv, ssm_conv_b, ssm_a_log, ssm_dt_bias, ssm_d, ssm_norm, w_br_dn, w_br_ssm, w_br_attn, w_out, ffn_norm, w_ff_gate, w_ff_up, w_ff_down, w_router, w_moe_gate, w_moe_up, w_moe_down, ple_norm, w_ple, w_ple_gate, final_norm):
    bsz, s, d = x.shape
    t = bsz * s
    depth = w_in.shape[0]
    x = x.reshape(t, d)
    for i in range(depth):
        w_dn, w_ssm, w_attn, w_gate, w_small = _split_w_in(w_in[i])
        g = mix_norm[i]
        pr_dn = norm_matmul(x, g, w_dn, 512)
        pr_ssm, small = norm_matmul(x, g, w_ssm, 512, w_small=w_small)
        pr_attn = norm_matmul(x, g, w_attn, 768)
        gates = norm_matmul(x, g, w_gate, 512)

        y_dn = gated_deltanet(pr_dn, small, dn_conv[i], dn_a_log[i], dn_dt_bias[i], dn_norm[i], bsz, s)
        y_ssm = mamba2_ssd(pr_ssm, small, ssm_conv[i], ssm_conv_b[i], ssm_a_log[i], ssm_dt_bias[i], ssm_d[i], ssm_norm[i], bsz, s)
        y_attn = dilated_attention(pr_attn, bsz, s)

        x = merge_out(x, y_dn, y_ssm, y_attn, gates,
                      w_br_dn[i].astype(BF16), w_br_ssm[i].astype(BF16), w_br_attn[i].astype(BF16), w_out[i].astype(BF16))

        j = i // 2
        fin = final_norm if i == depth - 1 else None
        common = (p[i].reshape(t, -1), ple_norm[i], w_ple_gate[i].astype(BF16), w_ple[i].astype(BF16))
        if i % 2 == 0:
            x = ffn_ple(x, ffn_norm[i], w_ff_gate[j][None].astype(BF16), w_ff_up[j][None].astype(BF16),
                        w_ff_down[j][None].astype(BF16), *common, tm=1024, sub=256, final_gain=fin)
        else:
            wr = jnp.pad(w_router[j], ((0, 0), (0, LANES - N_EXPERTS)))
            x = moe_ple(x, ffn_norm[i], wr, w_moe_gate[j].astype(BF16), w_moe_up[j].astype(BF16), w_moe_down[j].astype(BF16),
                        *common, final_gain=fin)
    return x.reshape(bsz, s, d)
```

```python
import functools

import numpy as np
import jax
import jax.numpy as jnp
from jax import lax
from jax.experimental import pallas as pl
from jax.experimental.pallas import tpu as pltpu

F32 = jnp.float32
BF16 = jnp.bfloat16

D_MODEL = 1024
PLE_DIM = 256
CONV_WIDTH = 4
RMS_EPS = 1e-6
DN_HEADS = 6
DN_HEAD_DIM = 128
DN_WIDTH = DN_HEADS * DN_HEAD_DIM
SSM_HEADS = 12
SSM_HEAD_DIM = 64
SSM_WIDTH = SSM_HEADS * SSM_HEAD_DIM
SSM_GROUPS = 2
SSM_STATE = 128
SSM_XBC = SSM_WIDTH + 2 * SSM_GROUPS * SSM_STATE
SSM_GW = SSM_WIDTH // SSM_GROUPS
SSM_HPG = SSM_HEADS // SSM_GROUPS
ATTN_HEADS = 12
ATTN_HEAD_DIM = 64
ATTN_QKV = ATTN_HEADS * ATTN_HEAD_DIM
DILATION_GROUPS = ((128, 1), (512, 4), (2048, 16))
N_DIL = len(DILATION_GROUPS)
ATTN_GROUP_HEADS = ATTN_HEADS // N_DIL
ATTN_OUT_WIDTH = ATTN_GROUP_HEADS * ATTN_HEAD_DIM
ATTN_BLOCK = 128
ALIBI_MAX_BIAS = 8.0
N_BRANCHES = 3
FFN_DIM = 2816
N_EXPERTS = 8
EXPERT_DIM = 3584
IN_SPLIT_SIZES = (3 * DN_WIDTH, DN_WIDTH, DN_HEADS, DN_HEADS, SSM_XBC, SSM_WIDTH, SSM_HEADS, 3 * ATTN_QKV, N_BRANCHES * D_MODEL)

LANES = 128
HALVES = ATTN_OUT_WIDTH // LANES
HEADS_PER_HALF = LANES // ATTN_HEAD_DIM
DT_LANE0 = 2 * DN_HEADS
CHUNK = 128
DN_HEADS_PER_STEP = 2
DN_UNROLL = 16
DN_STEP_W = DN_HEADS_PER_STEP * DN_HEAD_DIM
SSM_CONV_ROWS = 256
ATTN_LOCKSTEP = 4
NEG_BIG = -1e30
VMEM_LIMIT_BYTES = 56 * 1024 * 1024


def _cparams(n_axes):
    return pltpu.CompilerParams(dimension_semantics=("arbitrary",) * n_axes, vmem_limit_bytes=VMEM_LIMIT_BYTES)


def _rms(x, g):
    return x * lax.rsqrt(jnp.mean(x * x, axis=-1, keepdims=True) + RMS_EPS) * g


def _dot(a, b):
    return jnp.dot(a, b, preferred_element_type=F32)


def _dot_nt(a, b):
    return lax.dot_general(a, b, (((1,), (1,)), ((), ())), preferred_element_type=F32)


def _dot_tn(a, b):
    return lax.dot_general(a, b, (((0,), (0,)), ((), ())), preferred_element_type=F32)


def _silu(x):
    return x * jax.nn.sigmoid(x)


def _softplus(x):
    return jnp.maximum(x, 0.0) + jnp.log1p(jnp.exp(-jnp.abs(x)))


def _norm_matmul_body(x_ref, g_ref, w_ref, *rest, chunk, with_small):
    if with_small:
        ws_ref, o_ref, os_ref = rest
    else:
        (o_ref,) = rest
    hn = _rms(x_ref[...], g_ref[...]).astype(BF16)
    for c in range(w_ref.shape[1] // chunk):
        sl = slice(c * chunk, (c + 1) * chunk)
        o_ref[:, sl] = _dot(hn, w_ref[:, sl]).astype(BF16)
    if with_small:
        os_ref[...] = _dot(hn, ws_ref[...])


def norm_matmul(x, gain, w, chunk, w_small=None, tm=1024):
    t, d = x.shape
    n = w.shape[1]
    with_small = w_small is not None
    in_specs = [pl.BlockSpec((tm, d), lambda i: (i, 0)),
                pl.BlockSpec((1, d), lambda i: (0, 0)),
                pl.BlockSpec((d, n), lambda i: (0, 0), pipeline_mode=pl.Buffered(1))]
    out_specs = [pl.BlockSpec((tm, n), lambda i: (i, 0))]
    out_shape = [jax.ShapeDtypeStruct((t, n), BF16)]
    args = [x, gain.reshape(1, d), w]
    if with_small:
        in_specs.append(pl.BlockSpec((d, LANES), lambda i: (0, 0)))
        out_specs.append(pl.BlockSpec((tm, LANES), lambda i: (i, 0)))
        out_shape.append(jax.ShapeDtypeStruct((t, LANES), F32))
        args.append(w_small)
    outs = pl.pallas_call(
        functools.partial(_norm_matmul_body, chunk=chunk, with_small=with_small),
        grid=(t // tm,), in_specs=in_specs, out_specs=out_specs, out_shape=out_shape,
        compiler_params=_cparams(1), name="norm_matmul")(*args)
    return outs if with_small else outs[0]


def _merge_body(x_ref, ydn_ref, yssm_ref, yattn_ref, gate_ref, wdn_ref, wssm_ref, wattn_ref, wout_ref, o_ref):
    d = x_ref.shape[1]

    def sig(j):
        return jax.nn.sigmoid(gate_ref[:, j * d:(j + 1) * d].astype(F32))

    m = sig(0) * _dot(ydn_ref[...], wdn_ref[...])
    m = m + sig(1) * _dot(yssm_ref[...], wssm_ref[...])
    m = m + sig(2) * _dot(yattn_ref[...], wattn_ref[...])
    o_ref[...] = x_ref[...] + _dot(m.astype(BF16), wout_ref[...])


def merge_out(x, y_dn, y_ssm, y_attn, gates, w_dn, w_ssm, w_attn, w_out, tm=1024):
    t, d = x.shape

    def rows(width):
        return pl.BlockSpec((tm, width), lambda i: (i, 0))

    def whole(a):
        return pl.BlockSpec(a.shape, lambda i: (0, 0), pipeline_mode=pl.Buffered(1))

    return pl.pallas_call(
        _merge_body, grid=(t // tm,),
        in_specs=[rows(d), rows(y_dn.shape[1]), rows(y_ssm.shape[1]), rows(y_attn.shape[1]), rows(gates.shape[1]),
                  whole(w_dn), whole(w_ssm), whole(w_attn), whole(w_out)],
        out_specs=rows(d), out_shape=jax.ShapeDtypeStruct((t, d), F32),
        compiler_params=_cparams(1), name="merge_out")(x, y_dn, y_ssm, y_attn, gates, w_dn, w_ssm, w_attn, w_out)


def _swiglu(hn, wg_ref, wu_ref, wd_ref, sub):
    y = None
    for c in range(wg_ref.shape[2] // sub):
        sl = slice(c * sub, (c + 1) * sub)
        a = _dot(hn, wg_ref[0, :, sl])
        u = _dot(hn, wu_ref[0, :, sl])
        yc = _dot((_silu(a) * u).astype(BF16), wd_ref[0, sl, :])
        y = yc if y is None else y + yc
    return y


def _ple_epilogue(x1, p_ref, pg_ref, wpg_ref, wp_ref, fg_ref):
    gate = jax.nn.sigmoid(_dot(_rms(x1, pg_ref[...]).astype(BF16), wpg_ref[...]))
    x2 = x1 + _dot(p_ref[...].astype(BF16), wp_ref[...]) * gate
    return x2 if fg_ref is None else _rms(x2, fg_ref[...])


def _ffn_body(*refs, sub, final):
    it = iter(refs)
    x_ref, g_ref, wg_ref, wu_ref, wd_ref, p_ref, pg_ref, wpg_ref, wp_ref = (next(it) for _ in range(9))
    fg_ref = next(it) if final else None
    o_ref = next(it)
    x = x_ref[...]
    y = _swiglu(_rms(x, g_ref[...]).astype(BF16), wg_ref, wu_ref, wd_ref, sub)
    o_ref[...] = _ple_epilogue(x + y, p_ref, pg_ref, wpg_ref, wp_ref, fg_ref)


def ffn_ple(x, ffn_gain, wg, wu, wd, p, ple_gain, w_ple_gate, w_ple, *, tm, sub, final_gain=None):
    t, d = x.shape
    final = final_gain is not None

    def whole(a):
        return pl.BlockSpec(a.shape, lambda i: (0,) * a.ndim, pipeline_mode=pl.Buffered(1))

    in_specs = [pl.BlockSpec((tm, d), lambda i: (i, 0)), pl.BlockSpec((1, d), lambda i: (0, 0)),
                whole(wg), whole(wu), whole(wd),
                pl.BlockSpec((tm, p.shape[1]), lambda i: (i, 0)), pl.BlockSpec((1, d), lambda i: (0, 0)),
                whole(w_ple_gate), whole(w_ple)]
    args = [x, ffn_gain.reshape(1, d), wg, wu, wd, p, ple_gain.reshape(1, d), w_ple_gate, w_ple]
    if final:
        in_specs.append(pl.BlockSpec((1, d), lambda i: (0, 0)))
        args.append(final_gain.reshape(1, d))
    return pl.pallas_call(
        functools.partial(_ffn_body, sub=sub, final=final), grid=(t // tm,), in_specs=in_specs,
        out_specs=pl.BlockSpec((tm, d), lambda i: (i, 0)), out_shape=jax.ShapeDtypeStruct((t, d), F32),
        compiler_params=_cparams(1), name="ffn_ple")(*args)


ROUTE_I1, ROUTE_I2, ROUTE_W1, ROUTE_W2 = 8, 9, 10, 11
MOE_TILE = 512
MOE_ROWS_STEP = 512


def _route_body(x_ref, g_ref, wr_ref, o_ref):
    hn = _rms(x_ref[...], g_ref[...])
    logits = jnp.dot(hn, wr_ref[...], precision=lax.Precision.HIGHEST, preferred_element_type=F32)
    lane = lax.broadcasted_iota(jnp.int32, logits.shape, 1)
    lg = jnp.where(lane < N_EXPERTS, logits, NEG_BIG)
    m1 = jnp.max(lg, axis=1, keepdims=True)
    i1 = jnp.min(jnp.where(lg == m1, lane, LANES), axis=1, keepdims=True)
    lg2 = jnp.where(lane == i1, NEG_BIG, lg)
    m2 = jnp.max(lg2, axis=1, keepdims=True)
    i2 = jnp.min(jnp.where(lg2 == m2, lane, LANES), axis=1, keepdims=True)
    e2 = jnp.exp(m2 - m1)
    w1 = 1.0 / (1.0 + e2)
    rec = jnp.where(lane == ROUTE_I1, i1.astype(F32), 0.0) + jnp.where(lane == ROUTE_I2, i2.astype(F32), 0.0)
    o_ref[...] = rec + jnp.where(lane == ROUTE_W1, w1, 0.0) + jnp.where(lane == ROUTE_W2, e2 * w1, 0.0)


def moe_route(x, gain, w_router, tm=1024):
    t, d = x.shape
    return pl.pallas_call(
        _route_body, grid=(t // tm,),
        in_specs=[pl.BlockSpec((tm, d), lambda i: (i, 0)), pl.BlockSpec((1, d), lambda i: (0, 0)),
                  pl.BlockSpec((d, LANES), lambda i: (0, 0))],
        out_specs=pl.BlockSpec((tm, LANES), lambda i: (i, 0)), out_shape=jax.ShapeDtypeStruct((t, LANES), F32),
        compiler_params=_cparams(1), name="moe_route")(x, gain.reshape(1, d), w_router)


def _row_positions(route, n_rows_pad):
    t = route.shape[0]
    ids = route[:, ROUTE_I1:ROUTE_I2 + 1].astype(jnp.int32).reshape(-1)
    onehot = (ids[:, None] == jnp.arange(N_EXPERTS, dtype=jnp.int32)[None, :]).astype(jnp.int32)
    csum = jnp.cumsum(onehot, axis=0)
    rank = jnp.sum((csum - onehot) * onehot, axis=1)
    counts = csum[-1]
    sizes = (counts + MOE_TILE - 1) // MOE_TILE * MOE_TILE
    ends = jnp.cumsum(sizes)
    pos = (ends - sizes)[ids] + rank
    tile_start = jnp.arange(n_rows_pad // MOE_TILE, dtype=jnp.int32) * MOE_TILE
    tile_valid = (tile_start < ends[-1]).astype(jnp.int32)
    tile_expert = jnp.minimum(jnp.sum((tile_start[:, None] >= ends[None, :]).astype(jnp.int32), axis=1), N_EXPERTS - 1)
    last_expert = jnp.max(jnp.where(sizes > 0, jnp.arange(N_EXPERTS, dtype=jnp.int32), 0))
    tile_expert = jnp.where(tile_valid == 1, tile_expert, last_expert)
    return pos.reshape(t // MOE_ROWS_STEP, 1, 2 * MOE_ROWS_STEP), tile_expert, tile_valid


ROW_DMA_UNROLL = 8


def _row_copy_burst(n_rows, row_copy):
    def start(j, c):
        row_copy(j, 0).start(priority=0)
        row_copy(j, 1).start(priority=1)
        return c

    def wait(j, c):
        row_copy(j, 0).wait()
        row_copy(j, 1).wait()
        return c

    lax.fori_loop(0, n_rows, start, 0, unroll=ROW_DMA_UNROLL)
    lax.fori_loop(0, n_rows, wait, 0, unroll=ROW_DMA_UNROLL)


def _scatter_rows_body(pos_ref, x_ref, xs_init_ref, xs_ref, sem):
    del xs_init_ref

    def row_copy(j, k):
        return pltpu.make_async_copy(x_ref.at[pl.ds(j, 1), :], xs_ref.at[pl.ds(pos_ref[0, 0, 2 * j + k], 1), :], sem)

    _row_copy_burst(x_ref.shape[0], row_copy)


def scatter_rows(x, pos, n_rows_pad):
    t, d = x.shape
    tm = MOE_ROWS_STEP
    return pl.pallas_call(
        _scatter_rows_body, grid=(t // tm,),
        in_specs=[pl.BlockSpec((1, 1, 2 * tm), lambda i: (i, 0, 0), memory_space=pltpu.SMEM),
                  pl.BlockSpec((tm, d), lambda i: (i, 0)), pl.BlockSpec(memory_space=pl.ANY)],
        out_specs=pl.BlockSpec(memory_space=pl.ANY), out_shape=jax.ShapeDtypeStruct((n_rows_pad, d), F32),
        scratch_shapes=[pltpu.SemaphoreType.DMA(())], input_output_aliases={2: 0},
        compiler_params=_cparams(1), name="moe_scatter_rows")(pos, x, jnp.zeros((n_rows_pad, d), F32))


def _expert_body(te_ref, tv_ref, xs_ref, g_ref, wg_ref, wu_ref, wd_ref, o_ref, *, sub):
    del te_ref

    @pl.when(tv_ref[pl.program_id(0)] == 1)
    def _():
        o_ref[...] = _swiglu(_rms(xs_ref[...], g_ref[...]).astype(BF16), wg_ref, wu_ref, wd_ref, sub)

    @pl.when(tv_ref[pl.program_id(0)] == 0)
    def _():
        o_ref[...] = jnp.zeros_like(o_ref)


def grouped_swiglu(xs, gain, wg, wu, wd, tile_expert, tile_valid, sub=512):
    n_rows, d = xs.shape
    f_dim = wg.shape[2]
    grid_spec = pltpu.PrefetchScalarGridSpec(
        num_scalar_prefetch=2, grid=(n_rows // MOE_TILE,),
        in_specs=[pl.BlockSpec((MOE_TILE, d), lambda i, te, tv: (i, 0)),
                  pl.BlockSpec((1, d), lambda i, te, tv: (0, 0)),
                  pl.BlockSpec((1, d, f_dim), lambda i, te, tv: (te[i], 0, 0), pipeline_mode=pl.Buffered(1)),
                  pl.BlockSpec((1, d, f_dim), lambda i, te, tv: (te[i], 0, 0), pipeline_mode=pl.Buffered(1)),
                  pl.BlockSpec((1, f_dim, d), lambda i, te, tv: (te[i], 0, 0), pipeline_mode=pl.Buffered(1))],
        out_specs=pl.BlockSpec((MOE_TILE, d), lambda i, te, tv: (i, 0)))
    return pl.pallas_call(
        functools.partial(_expert_body, sub=sub), grid_spec=grid_spec,
        out_shape=jax.ShapeDtypeStruct((n_rows, d), F32),
        compiler_params=_cparams(1), name="moe_grouped_swiglu")(tile_expert, tile_valid, xs, gain.reshape(1, d), wg, wu, wd)


def _combine_body(*refs, final):
    it = iter(refs)
    pos_ref, x_ref, route_ref, p_ref, pg_ref, wpg_ref, wp_ref = (next(it) for _ in range(7))
    fg_ref = next(it) if final else None
    ys_ref, o_ref, buf, sem = (next(it) for _ in range(4))

    def row_copy(j, k):
        return pltpu.make_async_copy(ys_ref.at[pl.ds(pos_ref[0, 0, 2 * j + k], 1), :], buf.at[k, pl.ds(j, 1), :], sem)

    _row_copy_burst(x_ref.shape[0], row_copy)
    route = route_ref[...]
    moe = route[:, ROUTE_W1:ROUTE_W1 + 1] * buf[0] + route[:, ROUTE_W2:ROUTE_W2 + 1] * buf[1]
    o_ref[...] = _ple_epilogue(x_ref[...] + moe, p_ref, pg_ref, wpg_ref, wp_ref, fg_ref)


def combine_ple(x, route, pos, ys, p, ple_gain, w_ple_gate, w_ple, final_gain=None):
    t, d = x.shape
    tm = MOE_ROWS_STEP
    final = final_gain is not None
    in_specs = [pl.BlockSpec((1, 1, 2 * tm), lambda i: (i, 0, 0), memory_space=pltpu.SMEM),
                pl.BlockSpec((tm, d), lambda i: (i, 0)), pl.BlockSpec((tm, LANES), lambda i: (i, 0)),
                pl.BlockSpec((tm, p.shape[1]), lambda i: (i, 0)), pl.BlockSpec((1, d), lambda i: (0, 0)),
                pl.BlockSpec((d, d), lambda i: (0, 0), pipeline_mode=pl.Buffered(1)),
                pl.BlockSpec((p.shape[1], d), lambda i: (0, 0), pipeline_mode=pl.Buffered(1))]
    args = [pos, x, route, p, ple_gain.reshape(1, d), w_ple_gate, w_ple]
    if final:
        in_specs.append(pl.BlockSpec((1, d), lambda i: (0, 0)))
        args.append(final_gain.reshape(1, d))
    in_specs.append(pl.BlockSpec(memory_space=pl.ANY))
    args.append(ys)
    return pl.pallas_call(
        functools.partial(_combine_body, final=final), grid=(t // tm,), in_specs=in_specs,
        out_specs=pl.BlockSpec((tm, d), lambda i: (i, 0)), out_shape=jax.ShapeDtypeStruct((t, d), F32),
        scratch_shapes=[pltpu.VMEM((2, tm, d), F32), pltpu.SemaphoreType.DMA(())],
        compiler_params=_cparams(1), name="moe_combine_ple")(*args)


def moe_ple(x, ffn_gain, w_router, wg, wu, wd, p, ple_gain, w_ple_gate, w_ple, final_gain=None):
    t = x.shape[0]
    n_rows_pad = 2 * t + N_EXPERTS * MOE_TILE
    route = moe_route(x, ffn_gain, w_router)
    pos, tile_expert, tile_valid = _row_positions(route, n_rows_pad)
    xs = scatter_rows(x, pos, n_rows_pad)
    ys = grouped_swiglu(xs, ffn_gain, wg, wu, wd, tile_expert, tile_valid)
    return combine_ple(x, route, pos, ys, p, ple_gain, w_ple_gate, w_ple, final_gain)


CONV_PAD = 8


def _causal_conv_silu(xf, w_ref, pad_ref, bias=None):
    k_w = w_ref.shape[0]
    s = xf.shape[0]
    pad_ref[0:CONV_PAD, :] = jnp.zeros((CONV_PAD, xf.shape[1]), F32)
    pad_ref[CONV_PAD:CONV_PAD + s, :] = xf
    y = xf * w_ref[k_w - 1:k_w, :]
    for j in range(k_w - 1):
        start = CONV_PAD - (k_w - 1 - j)
        y = y + pad_ref[start:start + s, :] * w_ref[j:j + 1, :]
    if bias is not None:
        y = y + bias
    return _silu(y)


def _chunk_cumsum(x, rows):
    s = 1
    while s < CHUNK:
        x = x + jnp.where(rows % CHUNK >= s, pltpu.roll(x, s, 0), 0.0)
        s *= 2
    return x


def _tri_iotas():
    return lax.broadcasted_iota(jnp.int32, (CHUNK, CHUNK), 0), lax.broadcasted_iota(jnp.int32, (CHUNK, CHUNK), 1)


def _decay_from(g_col, g_row, incl):
    return jnp.where(incl, jnp.exp(jnp.where(incl, g_col - g_row, 0.0)), 0.0)


def _decay_matrix(gm_b, incl):
    return _decay_from(gm_b, gm_b.T, incl)


def _unit_lower_inverses(mats, ri, ci):
    eye = (ri == ci).astype(F32)
    first = (ri - ci == 1) & (ri % 2 == 1)
    ts = [eye - jnp.where(first, a, 0.0) for a in mats]
    s = 2
    while s < CHUNK:
        off_mask = ((ri // s) % 2 == 1) & (ci // s == ri // s - 1)
        tbs = [t.astype(BF16) for t in ts]
        inner = [_dot(jnp.where(off_mask, a, 0.0).astype(BF16), tb).astype(BF16) for a, tb in zip(mats, tbs)]
        outer = [_dot(tb, x) for tb, x in zip(tbs, inner)]
        ts = [t - x for t, x in zip(ts, outer)]
        s *= 2
    return ts


def _dn_body(alog_ref, dtb_ref, q_ref, k_ref, v_ref, z_ref, sm_ref, cwq_ref, cwk_ref, cwv_ref, ng_ref, o_ref,
             qn_scr, kn_scr, vc_scr, gam_scr, beta_scr, p_scr, n_scr, qp_scr, o0_scr, gl_scr, pad_scr, gall_scr, ball_scr):
    s = q_ref.shape[0]
    hd = DN_HEAD_DIM
    n_chunks = s // CHUNK
    lane = lax.broadcasted_iota(jnp.int32, (s, LANES), 1)
    ri, ci = _tri_iotas()
    incl = ri >= ci
    strict = ri > ci

    def l2n(x):
        return x * lax.rsqrt(jnp.sum(x * x, axis=-1, keepdims=True) + RMS_EPS)

    @pl.when(pl.program_id(1) == 0)
    def _():
        rows = lax.broadcasted_iota(jnp.int32, (s, LANES), 0)
        sm = sm_ref[...]
        gall_scr[...] = _chunk_cumsum(-jnp.exp(alog_ref[...]) * _softplus(sm + dtb_ref[...]), rows)
        ball_scr[...] = jax.nn.sigmoid(sm)

    for hh in range(DN_HEADS_PER_STEP):
        h = pl.program_id(1) * DN_HEADS_PER_STEP + hh
        hs = slice(hh * hd, (hh + 1) * hd)
        qn_scr[...] = l2n(_causal_conv_silu(q_ref[:, hs].astype(F32), cwq_ref.at[:, hs], pad_scr)) * hd ** -0.5
        kn_scr[...] = l2n(_causal_conv_silu(k_ref[:, hs].astype(F32), cwk_ref.at[:, hs], pad_scr))
        vc_scr[...] = _causal_conv_silu(v_ref[:, hs].astype(F32), cwv_ref.at[:, hs], pad_scr)
        gam_scr[...] = jnp.broadcast_to(jnp.sum(jnp.where(lane == h, gall_scr[...], 0.0), axis=1, keepdims=True), (s, LANES))
        beta_scr[...] = jnp.broadcast_to(
            jnp.sum(jnp.where(lane == DN_HEADS + h, ball_scr[...], 0.0), axis=1, keepdims=True), (s, LANES))

        def factor_group(cg, carry, hh=hh):
            us = range(DN_UNROLL)
            sls = [pl.ds(pl.multiple_of((cg * DN_UNROLL + u) * CHUNK, CHUNK), CHUNK) for u in us]
            kc = [kn_scr[sl, :] for sl in sls]
            gm = [gam_scr[sl, :] for sl in sls]
            bt = [beta_scr[sl, :] for sl in sls]
            decay = [_decay_matrix(gm[u], incl) for u in us]
            kb = [kc[u] * bt[u] for u in us]
            kcb = [kc[u].astype(BF16) for u in us]
            kk = [_dot_nt(kb[u].astype(BF16), kcb[u]) for u in us]
            qk = [_dot_nt(qn_scr[sls[u], :].astype(BF16), kcb[u]) for u in us]
            t_inv = _unit_lower_inverses([jnp.where(strict, kk[u] * decay[u], 0.0) for u in us], ri, ci)
            eg = [jnp.exp(gm[u]) for u in us]
            rhs = [jnp.concatenate([kb[u] * eg[u], vc_scr[sls[u], :] * bt[u]], axis=1).astype(BF16) for u in us]
            wu = [_dot(t_inv[u].astype(BF16), rhs[u]).astype(BF16) for u in us]
            g_last = [gm[u][CHUNK - 1:CHUNK, :] for u in us]
            pn = [_dot_tn((kc[u] * jnp.exp(g_last[u] - gm[u])).astype(BF16), wu[u]) for u in us]
            qo = [_dot((qk[u] * decay[u]).astype(BF16), wu[u]) for u in us]
            for u in us:
                p_scr[hh, sls[u], :] = pn[u][:, :hd].astype(BF16)
                n_scr[hh, sls[u], :] = pn[u][:, hd:]
                qp_scr[hh, sls[u], :] = (qn_scr[sls[u], :] * eg[u] - qo[u][:, :hd]).astype(BF16)
                o0_scr[hh, sls[u], :] = qo[u][:, hd:]
                gl_scr[hh, pl.ds(pl.multiple_of((cg * DN_UNROLL + u) * 8, 8), 8), :] = jnp.broadcast_to(
                    jnp.exp(g_last[u]), (8, LANES))
            return carry

        lax.fori_loop(0, n_chunks // DN_UNROLL, factor_group, 0)

    def scan(c, states):
        sl = pl.ds(pl.multiple_of(c * CHUNK, CHUNK), CHUNK)
        new = []
        for hh in range(DN_HEADS_PER_STEP):
            hs = slice(hh * hd, (hh + 1) * hd)
            state = states[hh]
            sb = state.astype(BF16)
            o = _dot(qp_scr[hh, sl, :], sb) + o0_scr[hh, sl, :]
            e_last = gl_scr[hh, pl.ds(pl.multiple_of(c * 8, 8), 1), :]
            new.append(state * e_last - _dot(p_scr[hh, sl, :], sb) + n_scr[hh, sl, :])
            on = o * lax.rsqrt(jnp.mean(o * o, axis=-1, keepdims=True) + RMS_EPS) * ng_ref[...]
            o_ref[sl, hs] = (on * _silu(z_ref[sl, hs].astype(F32))).astype(BF16)
        return tuple(new)

    lax.fori_loop(0, n_chunks, scan, tuple(jnp.zeros((hd, hd), F32) for _ in range(DN_HEADS_PER_STEP)))


def gated_deltanet(pr_dn, small, conv_w, a_log, dt_bias, norm_g, bsz, s):
    t = bsz * s
    hd = DN_HEAD_DIM
    n_steps = DN_HEADS // DN_HEADS_PER_STEP
    assert (s // CHUNK) % DN_UNROLL == 0

    def col(j0):
        return pl.BlockSpec((s, DN_STEP_W), lambda b, h, j0=j0: (b, j0 + h))

    def cw(j0):
        return pl.BlockSpec((CONV_WIDTH, DN_STEP_W), lambda b, h, j0=j0: (0, j0 + h))

    def per_head(dtype):
        return pltpu.VMEM((DN_HEADS_PER_STEP, s, hd), dtype)

    def lanes(v):
        return jnp.zeros((1, LANES), F32).at[0, :DN_HEADS].set(v)

    lane_vec = pl.BlockSpec((1, LANES), lambda b, h: (0, 0))
    return pl.pallas_call(
        _dn_body, grid=(bsz, n_steps),
        in_specs=[lane_vec, lane_vec, col(0), col(n_steps), col(2 * n_steps), col(3 * n_steps),
                  pl.BlockSpec((s, LANES), lambda b, h: (b, 0)),
                  cw(0), cw(n_steps), cw(2 * n_steps), pl.BlockSpec((1, hd), lambda b, h: (0, 0))],
        out_specs=pl.BlockSpec((s, DN_STEP_W), lambda b, h: (b, h)),
        out_shape=jax.ShapeDtypeStruct((t, DN_WIDTH), BF16),
        scratch_shapes=[pltpu.VMEM((s, hd), F32)] * 5 + [per_head(BF16), per_head(F32), per_head(BF16), per_head(F32),
                                                        pltpu.VMEM((DN_HEADS_PER_STEP, s // CHUNK * 8, LANES), F32),
                                                        pltpu.VMEM((CONV_PAD + s, hd), F32),
                                                        pltpu.VMEM((s, LANES), F32), pltpu.VMEM((s, LANES), F32)],
        compiler_params=_cparams(2), name="gated_deltanet")(
            lanes(a_log), lanes(dt_bias), pr_dn, pr_dn, pr_dn, pr_dn, small, conv_w, conv_w, conv_w, norm_g.reshape(1, hd))


def _expand_heads(v, g, n_rows):
    lane_head = lax.broadcasted_iota(jnp.int32, (n_rows, SSM_GW), 1) // SSM_HEAD_DIM
    out = jnp.zeros((n_rows, SSM_GW), F32)
    for hh in range(SSM_HPG):
        l0 = DT_LANE0 + g * SSM_HPG + hh
        out = jnp.where(lane_head == hh, v[:, l0:l0 + 1], out)
    return out


def _ssd_body(pr_ref, sm_ref, cw_ref, cb_ref, alog_ref, dtb_ref, d_ref, ng_ref, o_ref,
              x_scr, b_scr, c_scr, dt_scr, acs_scr, st_scr, pad_scr):
    s = pr_ref.shape[0]
    rows = lax.broadcasted_iota(jnp.int32, (s, LANES), 0)
    n_state = SSM_GROUPS * SSM_STATE
    k_w = cw_ref.shape[0]
    nr = SSM_CONV_ROWS

    def conv_rows(i, carry):
        rsl = pl.ds(pl.multiple_of(i * nr, nr), nr)
        for j in range(SSM_XBC // LANES):
            cs = slice(j * LANES, (j + 1) * LANES)
            src = j * LANES if j * LANES < SSM_WIDTH else j * LANES + SSM_WIDTH
            xf = pr_ref[rsl, src:src + LANES].astype(F32)
            pad_scr[0:CONV_PAD, cs] = pad_scr[nr:nr + CONV_PAD, cs]
            pad_scr[CONV_PAD:CONV_PAD + nr, cs] = xf
            y = xf * cw_ref[k_w - 1:k_w, cs] + cb_ref[:, cs]
            for tap in range(k_w - 1):
                start = CONV_PAD - (k_w - 1 - tap)
                y = y + pad_scr[start:start + nr, cs] * cw_ref[tap:tap + 1, cs]
            y = _silu(y)
            if j * LANES < SSM_WIDTH:
                x_scr[rsl, cs] = y
            elif j * LANES < SSM_WIDTH + n_state:
                b_scr[rsl, j * LANES - SSM_WIDTH:(j + 1) * LANES - SSM_WIDTH] = y.astype(BF16)
            else:
                c_scr[rsl, j * LANES - SSM_WIDTH - n_state:(j + 1) * LANES - SSM_WIDTH - n_state] = y.astype(BF16)
        return carry

    pad_scr[nr:nr + CONV_PAD, :] = jnp.zeros((CONV_PAD, SSM_XBC), F32)
    lax.fori_loop(0, s // nr, conv_rows, 0)
    dt = _softplus(sm_ref[...] + dtb_ref[...])
    dt_scr[...] = dt
    acs_scr[...] = _chunk_cumsum(dt * (-jnp.exp(alog_ref[...])), rows)
    st_scr[...] = jnp.zeros_like(st_scr)

    ri, ci = _tri_iotas()
    incl = ri >= ci

    def chunk(c, carry):
        sl = pl.ds(pl.multiple_of(c * CHUNK, CHUNK), CHUNK)
        dt_c, acs_c = dt_scr[sl, :], acs_scr[sl, :]
        acs_last = acs_c[CHUNK - 1:CHUNK, :]
        acs_t = acs_c.T
        groups = range(SSM_GROUPS)
        bc = [b_scr[sl, g * SSM_STATE:(g + 1) * SSM_STATE] for g in groups]
        cc = [c_scr[sl, g * SSM_STATE:(g + 1) * SSM_STATE] for g in groups]
        xg = [x_scr[sl, g * SSM_GW:(g + 1) * SSM_GW] for g in groups]
        xdt = [xg[g] * _expand_heads(dt_c, g, CHUNK) for g in groups]
        acs_e = [_expand_heads(acs_c, g, CHUNK) for g in groups]
        last_e = [_expand_heads(acs_last, g, 1) for g in groups]
        cbm = [_dot_nt(cc[g], bc[g]) for g in groups]
        state = [st_scr[g] for g in groups]
        y_off = [_dot(cc[g], state[g].astype(BF16)) for g in groups]
        st_new = [_dot_tn(bc[g], (xdt[g] * jnp.exp(last_e[g] - acs_e[g])).astype(BF16)) for g in groups]
        y_diag = []
        for g in groups:
            xdt_b = xdt[g].astype(BF16)
            for hh in range(SSM_HPG):
                l0 = DT_LANE0 + g * SSM_HPG + hh
                lm = _decay_from(jnp.broadcast_to(acs_c[:, l0:l0 + 1], (CHUNK, CHUNK)),
                                 jnp.broadcast_to(acs_t[l0:l0 + 1, :], (CHUNK, CHUNK)), incl)
                y_diag.append(_dot((cbm[g] * lm).astype(BF16), xdt_b[:, hh * SSM_HEAD_DIM:(hh + 1) * SSM_HEAD_DIM]))
        for g in groups:
            gs = slice(g * SSM_GW, (g + 1) * SSM_GW)
            st_scr[g] = state[g] * jnp.exp(last_e[g]) + st_new[g]
            y = y_off[g] * jnp.exp(acs_e[g]) + _expand_heads(d_ref[...], g, 1) * xg[g]
            y = y + jnp.concatenate(y_diag[g * SSM_HPG:(g + 1) * SSM_HPG], axis=1)
            y = y * _silu(pr_ref[sl, SSM_WIDTH + g * SSM_GW:SSM_WIDTH + (g + 1) * SSM_GW].astype(F32))
            y = y * lax.rsqrt(jnp.mean(y * y, axis=-1, keepdims=True) + RMS_EPS) * ng_ref[:, gs]
            o_ref[sl, gs] = y.astype(BF16)
        return carry

    lax.fori_loop(0, s // CHUNK, chunk, 0)


def mamba2_ssd(pr_ssm, small, conv_w, conv_b, a_log, dt_bias, d_skip, norm_g, bsz, s):
    t = bsz * s

    def lanes(v):
        return jnp.zeros((1, LANES), F32).at[0, DT_LANE0:DT_LANE0 + SSM_HEADS].set(v)

    def whole(shape):
        return pl.BlockSpec(shape, lambda b: (0, 0))

    n_state = SSM_GROUPS * SSM_STATE
    return pl.pallas_call(
        _ssd_body, grid=(bsz,),
        in_specs=[pl.BlockSpec((s, pr_ssm.shape[1]), lambda b: (b, 0)),
                  pl.BlockSpec((s, LANES), lambda b: (b, 0)),
                  whole((CONV_WIDTH, SSM_XBC)), whole((1, SSM_XBC)), whole((1, LANES)), whole((1, LANES)), whole((1, LANES)),
                  whole((1, SSM_WIDTH))],
        out_specs=pl.BlockSpec((s, SSM_WIDTH), lambda b: (b, 0)),
        out_shape=jax.ShapeDtypeStruct((t, SSM_WIDTH), BF16),
        scratch_shapes=[pltpu.VMEM((s, SSM_WIDTH), F32), pltpu.VMEM((s, n_state), BF16), pltpu.VMEM((s, n_state), BF16),
                        pltpu.VMEM((s, LANES), F32), pltpu.VMEM((s, LANES), F32),
                        pltpu.VMEM((SSM_GROUPS, SSM_STATE, SSM_GW), F32),
                        pltpu.VMEM((CONV_PAD + SSM_CONV_ROWS, SSM_XBC), F32)],
        compiler_params=_cparams(1), name="mamba2_ssd")(
            pr_ssm, small, conv_w, conv_b.reshape(1, -1), lanes(a_log), lanes(dt_bias), lanes(d_skip), norm_g.reshape(1, -1))


def _attn_body(pr_ref, o_ref, qf, kf, vf, og, lg):
    s = pr_ref.shape[0]
    c = ATTN_BLOCK
    e = ATTN_HEAD_DIM
    ii = lax.broadcasted_iota(jnp.int32, (c, c), 0)
    jj = lax.broadcasted_iota(jnp.int32, (c, c), 1)
    lane_head = lax.broadcasted_iota(jnp.int32, (c, LANES), 1) // e
    ones_b = jnp.ones((c, LANES), BF16)
    for gi, (window, dil) in enumerate(DILATION_GROUPS):
        w_sub = window // dil
        nb = s // dil // c
        for j in range(HALVES):
            c0 = gi * ATTN_OUT_WIDTH + j * LANES
            qf[j] = pr_ref[:, c0:c0 + LANES].astype(F32) * e ** -0.5
            kf[j] = pr_ref[:, ATTN_QKV + c0:ATTN_QKV + c0 + LANES].astype(F32)
            vf[j] = pr_ref[:, 2 * ATTN_QKV + c0:2 * ATTN_QKV + c0 + LANES].astype(F32)
        d_cur = ii - jj
        d_prev = d_cur + c
        ok_cur = (d_cur >= 0) & (d_cur <= w_sub)
        ok_prev_static = d_prev <= w_sub

        def blocks(it, carry, gi=gi, dil=dil, nb=nb, d_cur=d_cur, d_prev=d_prev, ok_cur=ok_cur, ok_prev_static=ok_prev_static):
            cur, prev, ok_prev = [], [], []
            for b in range(ATTN_LOCKSTEP):
                idx = it * ATTN_LOCKSTEP + b
                r, n = idx // nb, idx % nb
                cur.append(pl.ds(r + n * (c * dil), c, stride=dil))
                prev.append(pl.ds(r + jnp.maximum(n - 1, 0) * (c * dil), c, stride=dil))
                ok_prev.append(ok_prev_static & (n > 0))
            units = [(b, j, hh) for b in range(ATTN_LOCKSTEP) for j in range(HALVES) for hh in range(HEADS_PER_HALF)]
            vcur = [[vf.at[j][cur[b], :].astype(BF16) for j in range(HALVES)] for b in range(ATTN_LOCKSTEP)]
            vprev = [[vf.at[j][prev[b], :].astype(BF16) for j in range(HALVES)] for b in range(ATTN_LOCKSTEP)]
            s_cur, s_prev = [], []
            for b in range(ATTN_LOCKSTEP):
                for j in range(HALVES):
                    q = qf.at[j][cur[b], :]
                    kc, kp = kf.at[j][cur[b], :].astype(BF16), kf.at[j][prev[b], :].astype(BF16)
                    for hh in range(HEADS_PER_HALF):
                        qh = jnp.where(lane_head == hh, q, 0.0).astype(BF16)
                        s_cur.append(_dot_nt(qh, kc))
                        s_prev.append(_dot_nt(qh, kp))
            probs, maxes = [], []
            for i, (b, j, hh) in enumerate(units):
                head = gi * ATTN_GROUP_HEADS + j * HEADS_PER_HALF + hh
                slope = float(2.0 ** (-ALIBI_MAX_BIAS * (head + 1.0) / ATTN_HEADS)) * dil
                sc = jnp.where(ok_cur, s_cur[i] - slope * d_cur.astype(F32), NEG_BIG)
                sp = jnp.where(ok_prev[b], s_prev[i] - slope * d_prev.astype(F32), NEG_BIG)
                m = jnp.max(jnp.maximum(sc, sp), axis=1, keepdims=True)
                probs.append((jnp.exp(sc - m).astype(BF16), jnp.exp(sp - m).astype(BF16)))
                maxes.append(m)
            nums = [_dot(probs[i][0], vcur[b][j]) + _dot(probs[i][1], vprev[b][j]) for i, (b, j, hh) in enumerate(units)]
            dens = [_dot(probs[i][0], ones_b) + _dot(probs[i][1], ones_b) for i in range(len(units))]
            for b in range(ATTN_LOCKSTEP):
                for j in range(HALVES):
                    i0 = (b * HALVES + j) * HEADS_PER_HALF
                    out, lse = nums[i0] / dens[i0], maxes[i0] + jnp.log(dens[i0])
                    for hh in range(1, HEADS_PER_HALF):
                        out = jnp.where(lane_head == hh, nums[i0 + hh] / dens[i0 + hh], out)
                        lse = jnp.where(lane_head == hh, maxes[i0 + hh] + jnp.log(dens[i0 + hh]), lse)
                    og.at[gi, j][cur[b], :] = out
                    lg.at[gi, j][cur[b], :] = lse
            return carry

        lax.fori_loop(0, dil * nb // ATTN_LOCKSTEP, blocks, 0)
    for j in range(HALVES):
        m = jnp.maximum(jnp.maximum(lg[0, j], lg[1, j]), lg[2, j])
        num = jnp.zeros((s, LANES), F32)
        den = jnp.zeros((s, LANES), F32)
        for gi in range(N_DIL):
            w = jnp.exp(lg[gi, j] - m)
            num = num + w * og[gi, j]
            den = den + w
        o_ref[:, j * LANES:(j + 1) * LANES] = (num / den).astype(BF16)


def dilated_attention(pr_attn, bsz, s):
    t = bsz * s
    assert N_DIL == 3
    for window, dil in DILATION_GROUPS:
        assert s % (dil * ATTN_BLOCK) == 0 and window // dil <= ATTN_BLOCK
    return pl.pallas_call(
        _attn_body, grid=(bsz,),
        in_specs=[pl.BlockSpec((s, pr_attn.shape[1]), lambda b: (b, 0))],
        out_specs=pl.BlockSpec((s, ATTN_OUT_WIDTH), lambda b: (b, 0)),
        out_shape=jax.ShapeDtypeStruct((t, ATTN_OUT_WIDTH), BF16),
        scratch_shapes=[pltpu.VMEM((HALVES, s, LANES), F32)] * 3 + [pltpu.VMEM((N_DIL, HALVES, s, LANES), F32)] * 2,
        compiler_params=_cparams(1), name="dilated_attention")(pr_attn)


def _split_w_in(w_in):
    cuts = np.cumsum(IN_SPLIT_SIZES)[:-1].tolist()
    dn_qkv, dn_z, dn_a, dn_b, ssm_xbc, ssm_z, ssm_dt, attn_qkv, gate = jnp.split(w_in, cuts, axis=-1)
    w_dn = jnp.concatenate([dn_qkv, dn_z], axis=1).astype(BF16)
    w_ssm = jnp.concatenate([ssm_xbc[:, :SSM_WIDTH], ssm_z, ssm_xbc[:, SSM_WIDTH:]], axis=1).astype(BF16)
    small = jnp.concatenate([dn_a, dn_b, ssm_dt], axis=1)
    w_small = jnp.pad(small, ((0, 0), (0, LANES - small.shape[1]))).astype(BF16)
    return w_dn, w_ssm, attn_qkv.astype(BF16), gate.astype(BF16), w_small


def kernel(x, p, mix_norm, w_in, dn_conv, dn_a_log, dn_dt_bias, dn_norm, ssm_conv, ssm_conv_b, ssm_a_log, ssm_dt_bias, ssm_d, ssm_norm, w_br_dn, w_br_ssm, w_br_attn, w_out, ffn_norm, w_ff_gate, w_ff_up, w_ff_down, w_router, w_moe_gate, w_moe_up, w_moe_down, ple_norm, w_ple, w_ple_gate, final_norm):
    bsz, s, d = x.shape
    t = bsz * s
    depth = w_in.shape[0]
    x = x.reshape(t, d)
    for i in range(depth):
        w_dn, w_ssm, w_attn, w_gate, w_small = _split_w_in(w_in[i])
        g = mix_norm[i]
        pr_dn = norm_matmul(x, g, w_dn, 512)
        pr_ssm, small = norm_matmul(x, g, w_ssm, 512, w_small=w_small)
        pr_attn = norm_matmul(x, g, w_attn, 768)
        gates = norm_matmul(x, g, w_gate, 512)

        y_dn = gated_deltanet(pr_dn, small, dn_conv[i], dn_a_log[i], dn_dt_bias[i], dn_norm[i], bsz, s)
        y_ssm = mamba2_ssd(pr_ssm, small, ssm_conv[i], ssm_conv_b[i], ssm_a_log[i], ssm_dt_bias[i], ssm_d[i], ssm_norm[i], bsz, s)
        y_attn = dilated_attention(pr_attn, bsz, s)

        x = merge_out(x, y_dn, y_ssm, y_attn, gates,
                      w_br_dn[i].astype(BF16), w_br_ssm[i].astype(BF16), w_br_attn[i].astype(BF16), w_out[i].astype(BF16))

        j = i // 2
        fin = final_norm if i == depth - 1 else None
        common = (p[i].reshape(t, -1), ple_norm[i], w_ple_gate[i].astype(BF16), w_ple[i].astype(BF16))
        if i % 2 == 0:
            x = ffn_ple(x, ffn_norm[i], w_ff_gate[j][None].astype(BF16), w_ff_up[j][None].astype(BF16),
                        w_ff_down[j][None].astype(BF16), *common, tm=1024, sub=256, final_gain=fin)
        else:
            wr = jnp.pad(w_router[j], ((0, 0), (0, LANES - N_EXPERTS)))
            x = moe_ple(x, ffn_norm[i], wr, w_moe_gate[j].astype(BF16), w_moe_up[j].astype(BF16), w_moe_down[j].astype(BF16),
                        *common, final_gain=fin)
    return x.reshape(bsz, s, d)
```

```python
import functools

import numpy as np
import jax
import jax.numpy as jnp
from jax import lax
from jax.experimental import pallas as pl
from jax.experimental.pallas import tpu as pltpu

F32 = jnp.float32
BF16 = jnp.bfloat16

D_MODEL = 1024
PLE_DIM = 256
CONV_WIDTH = 4
RMS_EPS = 1e-6
DN_HEADS = 6
DN_HEAD_DIM = 128
DN_WIDTH = DN_HEADS * DN_HEAD_DIM
SSM_HEADS = 12
SSM_HEAD_DIM = 64
SSM_WIDTH = SSM_HEADS * SSM_HEAD_DIM
SSM_GROUPS = 2
SSM_STATE = 128
SSM_XBC = SSM_WIDTH + 2 * SSM_GROUPS * SSM_STATE
SSM_GW = SSM_WIDTH // SSM_GROUPS
SSM_HPG = SSM_HEADS // SSM_GROUPS
ATTN_HEADS = 12
ATTN_HEAD_DIM = 64
ATTN_QKV = ATTN_HEADS * ATTN_HEAD_DIM
DILATION_GROUPS = ((128, 1), (512, 4), (2048, 16))
N_DIL = len(DILATION_GROUPS)
ATTN_GROUP_HEADS = ATTN_HEADS // N_DIL
ATTN_OUT_WIDTH = ATTN_GROUP_HEADS * ATTN_HEAD_DIM
ATTN_BLOCK = 128
ALIBI_MAX_BIAS = 8.0
N_BRANCHES = 3
FFN_DIM = 2816
N_EXPERTS = 8
EXPERT_DIM = 3584
IN_SPLIT_SIZES = (3 * DN_WIDTH, DN_WIDTH, DN_HEADS, DN_HEADS, SSM_XBC, SSM_WIDTH, SSM_HEADS, 3 * ATTN_QKV, N_BRANCHES * D_MODEL)

LANES = 128
HALVES = ATTN_OUT_WIDTH // LANES
HEADS_PER_HALF = LANES // ATTN_HEAD_DIM
DT_LANE0 = 2 * DN_HEADS
CHUNK = 128
DN_HEADS_PER_STEP = 2
DN_UNROLL = 16
DN_STEP_W = DN_HEADS_PER_STEP * DN_HEAD_DIM
SSM_CONV_ROWS = 256
ATTN_LOCKSTEP = 4
NEG_BIG = -1e30
VMEM_LIMIT_BYTES = 56 * 1024 * 1024


def _cparams(n_axes):
    return pltpu.CompilerParams(dimension_semantics=("arbitrary",) * n_axes, vmem_limit_bytes=VMEM_LIMIT_BYTES)


def _rms(x, g):
    return x * lax.rsqrt(jnp.mean(x * x, axis=-1, keepdims=True) + RMS_EPS) * g


def _dot(a, b):
    return jnp.dot(a, b, preferred_element_type=F32)


def _dot_nt(a, b):
    return lax.dot_general(a, b, (((1,), (1,)), ((), ())), preferred_element_type=F32)


def _dot_tn(a, b):
    return lax.dot_general(a, b, (((0,), (0,)), ((), ())), preferred_element_type=F32)


def _silu(x):
    return x * jax.nn.sigmoid(x)


def _softplus(x):
    return jnp.maximum(x, 0.0) + jnp.log1p(jnp.exp(-jnp.abs(x)))


def _norm_matmul_body(x_ref, g_ref, w_ref, *rest, chunk, with_small):
    if with_small:
        ws_ref, o_ref, os_ref = rest
    else:
        (o_ref,) = rest
    hn = _rms(x_ref[...], g_ref[...]).astype(BF16)
    for c in range(w_ref.shape[1] // chunk):
        sl = slice(c * chunk, (c + 1) * chunk)
        o_ref[:, sl] = _dot(hn, w_ref[:, sl]).astype(BF16)
    if with_small:
        os_ref[...] = _dot(hn, ws_ref[...])


def norm_matmul(x, gain, w, chunk, w_small=None, tm=1024):
    t, d = x.shape
    n = w.shape[1]
    with_small = w_small is not None
    in_specs = [pl.BlockSpec((tm, d), lambda i: (i, 0)),
                pl.BlockSpec((1, d), lambda i: (0, 0)),
                pl.BlockSpec((d, n), lambda i: (0, 0), pipeline_mode=pl.Buffered(1))]
    out_specs = [pl.BlockSpec((tm, n), lambda i: (i, 0))]
    out_shape = [jax.ShapeDtypeStruct((t, n), BF16)]
    args = [x, gain.reshape(1, d), w]
    if with_small:
        in_specs.append(pl.BlockSpec((d, LANES), lambda i: (0, 0)))
        out_specs.append(pl.BlockSpec((tm, LANES), lambda i: (i, 0)))
        out_shape.append(jax.ShapeDtypeStruct((t, LANES), F32))
        args.append(w_small)
    outs = pl.pallas_call(
        functools.partial(_norm_matmul_body, chunk=chunk, with_small=with_small),
        grid=(t // tm,), in_specs=in_specs, out_specs=out_specs, out_shape=out_shape,
        compiler_params=_cparams(1), name="norm_matmul")(*args)
    return outs if with_small else outs[0]


def _merge_body(x_ref, ydn_ref, yssm_ref, yattn_ref, gate_ref, wdn_ref, wssm_ref, wattn_ref, wout_ref, o_ref):
    d = x_ref.shape[1]

    def sig(j):
        return jax.nn.sigmoid(gate_ref[:, j * d:(j + 1) * d].astype(F32))

    m = sig(0) * _dot(ydn_ref[...], wdn_ref[...])
    m = m + sig(1) * _dot(yssm_ref[...], wssm_ref[...])
    m = m + sig(2) * _dot(yattn_ref[...], wattn_ref[...])
    o_ref[...] = x_ref[...] + _dot(m.astype(BF16), wout_ref[...])


def merge_out(x, y_dn, y_ssm, y_attn, gates, w_dn, w_ssm, w_attn, w_out, tm=1024):
    t, d = x.shape

    def rows(width):
        return pl.BlockSpec((tm, width), lambda i: (i, 0))

    def whole(a):
        return pl.BlockSpec(a.shape, lambda i: (0, 0), pipeline_mode=pl.Buffered(1))

    return pl.pallas_call(
        _merge_body, grid=(t // tm,),
        in_specs=[rows(d), rows(y_dn.shape[1]), rows(y_ssm.shape[1]), rows(y_attn.shape[1]), rows(gates.shape[1]),
                  whole(w_dn), whole(w_ssm), whole(w_attn), whole(w_out)],
        out_specs=rows(d), out_shape=jax.ShapeDtypeStruct((t, d), F32),
        compiler_params=_cparams(1), name="merge_out")(x, y_dn, y_ssm, y_attn, gates, w_dn, w_ssm, w_attn, w_out)


def _swiglu(hn, wg_ref, wu_ref, wd_ref, sub):
    y = None
    for c in range(wg_ref.shape[2] // sub):
        sl = slice(c * sub, (c + 1) * sub)
        a = _dot(hn, wg_ref[0, :, sl])
        u = _dot(hn, wu_ref[0, :, sl])
        yc = _dot((_silu(a) * u).astype(BF16), wd_ref[0, sl, :])
        y = yc if y is None else y + yc
    return y


def _ple_epilogue(x1, p_ref, pg_ref, wpg_ref, wp_ref, fg_ref):
    gate = jax.nn.sigmoid(_dot(_rms(x1, pg_ref[...]).astype(BF16), wpg_ref[...]))
    x2 = x1 + _dot(p_ref[...].astype(BF16), wp_ref[...]) * gate
    return x2 if fg_ref is None else _rms(x2, fg_ref[...])


def _ffn_body(*refs, sub, final):
    it = iter(refs)
    x_ref, g_ref, wg_ref, wu_ref, wd_ref, p_ref, pg_ref, wpg_ref, wp_ref = (next(it) for _ in range(9))
    fg_ref = next(it) if final else None
    o_ref = next(it)
    x = x_ref[...]
    y = _swiglu(_rms(x, g_ref[...]).astype(BF16), wg_ref, wu_ref, wd_ref, sub)
    o_ref[...] = _ple_epilogue(x + y, p_ref, pg_ref, wpg_ref, wp_ref, fg_ref)


def ffn_ple(x, ffn_gain, wg, wu, wd, p, ple_gain, w_ple_gate, w_ple, *, tm, sub, final_gain=None):
    t, d = x.shape
    final = final_gain is not None

    def whole(a):
        return pl.BlockSpec(a.shape, lambda i: (0,) * a.ndim, pipeline_mode=pl.Buffered(1))

    in_specs = [pl.BlockSpec((tm, d), lambda i: (i, 0)), pl.BlockSpec((1, d), lambda i: (0, 0)),
                whole(wg), whole(wu), whole(wd),
                pl.BlockSpec((tm, p.shape[1]), lambda i: (i, 0)), pl.BlockSpec((1, d), lambda i: (0, 0)),
                whole(w_ple_gate), whole(w_ple)]
    args = [x, ffn_gain.reshape(1, d), wg, wu, wd, p, ple_gain.reshape(1, d), w_ple_gate, w_ple]
    if final:
        in_specs.append(pl.BlockSpec((1, d), lambda i: (0, 0)))
        args.append(final_gain.reshape(1, d))
    return pl.pallas_call(
        functools.partial(_ffn_body, sub=sub, final=final), grid=(t // tm,), in_specs=in_specs,
        out_specs=pl.BlockSpec((tm, d), lambda i: (i, 0)), out_shape=jax.ShapeDtypeStruct((t, d), F32),
        compiler_params=_cparams(1), name="ffn_ple")(*args)


ROUTE_I1, ROUTE_I2, ROUTE_W1, ROUTE_W2 = 8, 9, 10, 11
MOE_TILE = 512
MOE_ROWS_STEP = 512


def _route_body(x_ref, g_ref, wr_ref, o_ref):
    hn = _rms(x_ref[...], g_ref[...])
    w = wr_ref[...]
    hn_hi, w_hi = hn.astype(BF16), w.astype(BF16)
    hn_lo, w_lo = (hn - hn_hi.astype(F32)).astype(BF16), (w - w_hi.astype(F32)).astype(BF16)
    logits = _dot(hn_hi, w_hi) + (_dot(hn_hi, w_lo) + _dot(hn_lo, w_hi))
    lane = lax.broadcasted_iota(jnp.int32, logits.shape, 1)
    lg = jnp.where(lane < N_EXPERTS, logits, NEG_BIG)
    m1 = jnp.max(lg, axis=1, keepdims=True)
    i1 = jnp.min(jnp.where(lg == m1, lane, LANES), axis=1, keepdims=True)
    lg2 = jnp.where(lane == i1, NEG_BIG, lg)
    m2 = jnp.max(lg2, axis=1, keepdims=True)
    i2 = jnp.min(jnp.where(lg2 == m2, lane, LANES), axis=1, keepdims=True)
    e2 = jnp.exp(m2 - m1)
    w1 = 1.0 / (1.0 + e2)
    rec = jnp.where(lane == ROUTE_I1, i1.astype(F32), 0.0) + jnp.where(lane == ROUTE_I2, i2.astype(F32), 0.0)
    o_ref[...] = rec + jnp.where(lane == ROUTE_W1, w1, 0.0) + jnp.where(lane == ROUTE_W2, e2 * w1, 0.0)


def moe_route(x, gain, w_router, tm=1024):
    t, d = x.shape
    return pl.pallas_call(
        _route_body, grid=(t // tm,),
        in_specs=[pl.BlockSpec((tm, d), lambda i: (i, 0)), pl.BlockSpec((1, d), lambda i: (0, 0)),
                  pl.BlockSpec((d, LANES), lambda i: (0, 0))],
        out_specs=pl.BlockSpec((tm, LANES), lambda i: (i, 0)), out_shape=jax.ShapeDtypeStruct((t, LANES), F32),
        compiler_params=_cparams(1), name="moe_route")(x, gain.reshape(1, d), w_router)


def _row_positions(route, n_rows_pad):
    t = route.shape[0]
    ids = route[:, ROUTE_I1:ROUTE_I2 + 1].astype(jnp.int32).reshape(-1)
    onehot = (ids[:, None] == jnp.arange(N_EXPERTS, dtype=jnp.int32)[None, :]).astype(jnp.int32)
    csum = jnp.cumsum(onehot, axis=0)
    rank = jnp.sum((csum - onehot) * onehot, axis=1)
    counts = csum[-1]
    sizes = (counts + MOE_TILE - 1) // MOE_TILE * MOE_TILE
    ends = jnp.cumsum(sizes)
    pos = (ends - sizes)[ids] + rank
    tile_start = jnp.arange(n_rows_pad // MOE_TILE, dtype=jnp.int32) * MOE_TILE
    tile_valid = (tile_start < ends[-1]).astype(jnp.int32)
    tile_expert = jnp.minimum(jnp.sum((tile_start[:, None] >= ends[None, :]).astype(jnp.int32), axis=1), N_EXPERTS - 1)
    last_expert = jnp.max(jnp.where(sizes > 0, jnp.arange(N_EXPERTS, dtype=jnp.int32), 0))
    tile_expert = jnp.where(tile_valid == 1, tile_expert, last_expert)
    return pos.reshape(t // MOE_ROWS_STEP, 1, 2 * MOE_ROWS_STEP), tile_expert, tile_valid


ROW_DMA_UNROLL = 8


def _row_copy_burst(n_rows, row_copy):
    def start(j, c):
        row_copy(j, 0).start(priority=0)
        row_copy(j, 1).start(priority=1)
        return c

    def wait(j, c):
        row_copy(j, 0).wait()
        row_copy(j, 1).wait()
        return c

    lax.fori_loop(0, n_rows, start, 0, unroll=ROW_DMA_UNROLL)
    lax.fori_loop(0, n_rows, wait, 0, unroll=ROW_DMA_UNROLL)


def _scatter_rows_body(pos_ref, x_ref, xs_init_ref, xs_ref, sem):
    del xs_init_ref

    def row_copy(j, k):
        return pltpu.make_async_copy(x_ref.at[pl.ds(j, 1), :], xs_ref.at[pl.ds(pos_ref[0, 0, 2 * j + k], 1), :], sem)

    _row_copy_burst(x_ref.shape[0], row_copy)


def scatter_rows(x, pos, n_rows_pad):
    t, d = x.shape
    tm = MOE_ROWS_STEP
    return pl.pallas_call(
        _scatter_rows_body, grid=(t // tm,),
        in_specs=[pl.BlockSpec((1, 1, 2 * tm), lambda i: (i, 0, 0), memory_space=pltpu.SMEM),
                  pl.BlockSpec((tm, d), lambda i: (i, 0)), pl.BlockSpec(memory_space=pl.ANY)],
        out_specs=pl.BlockSpec(memory_space=pl.ANY), out_shape=jax.ShapeDtypeStruct((n_rows_pad, d), F32),
        scratch_shapes=[pltpu.SemaphoreType.DMA(())], input_output_aliases={2: 0},
        compiler_params=_cparams(1), name="moe_scatter_rows")(pos, x, jnp.zeros((n_rows_pad, d), F32))


def _expert_body(te_ref, tv_ref, xs_ref, g_ref, wg_ref, wu_ref, wd_ref, o_ref, *, sub):
    del te_ref

    @pl.when(tv_ref[pl.program_id(0)] == 1)
    def _():
        o_ref[...] = _swiglu(_rms(xs_ref[...], g_ref[...]).astype(BF16), wg_ref, wu_ref, wd_ref, sub)

    @pl.when(tv_ref[pl.program_id(0)] == 0)
    def _():
        o_ref[...] = jnp.zeros_like(o_ref)


def grouped_swiglu(xs, gain, wg, wu, wd, tile_expert, tile_valid, sub=512):
    n_rows, d = xs.shape
    f_dim = wg.shape[2]
    grid_spec = pltpu.PrefetchScalarGridSpec(
        num_scalar_prefetch=2, grid=(n_rows // MOE_TILE,),
        in_specs=[pl.BlockSpec((MOE_TILE, d), lambda i, te, tv: (i, 0)),
                  pl.BlockSpec((1, d), lambda i, te, tv: (0, 0)),
                  pl.BlockSpec((1, d, f_dim), lambda i, te, tv: (te[i], 0, 0), pipeline_mode=pl.Buffered(1)),
                  pl.BlockSpec((1, d, f_dim), lambda i, te, tv: (te[i], 0, 0), pipeline_mode=pl.Buffered(1)),
                  pl.BlockSpec((1, f_dim, d), lambda i, te, tv: (te[i], 0, 0), pipeline_mode=pl.Buffered(1))],
        out_specs=pl.BlockSpec((MOE_TILE, d), lambda i, te, tv: (i, 0)))
    return pl.pallas_call(
        functools.partial(_expert_body, sub=sub), grid_spec=grid_spec,
        out_shape=jax.ShapeDtypeStruct((n_rows, d), F32),
        compiler_params=_cparams(1), name="moe_grouped_swiglu")(tile_expert, tile_valid, xs, gain.reshape(1, d), wg, wu, wd)


def _combine_body(*refs, final):
    it = iter(refs)
    pos_ref, x_ref, route_ref, p_ref, pg_ref, wpg_ref, wp_ref = (next(it) for _ in range(7))
    fg_ref = next(it) if final else None
    ys_ref, o_ref, buf, sem = (next(it) for _ in range(4))

    def row_copy(j, k):
        return pltpu.make_async_copy(ys_ref.at[pl.ds(pos_ref[0, 0, 2 * j + k], 1), :], buf.at[k, pl.ds(j, 1), :], sem)

    _row_copy_burst(x_ref.shape[0], row_copy)
    route = route_ref[...]
    moe = route[:, ROUTE_W1:ROUTE_W1 + 1] * buf[0] + route[:, ROUTE_W2:ROUTE_W2 + 1] * buf[1]
    o_ref[...] = _ple_epilogue(x_ref[...] + moe, p_ref, pg_ref, wpg_ref, wp_ref, fg_ref)


def combine_ple(x, route, pos, ys, p, ple_gain, w_ple_gate, w_ple, final_gain=None):
    t, d = x.shape
    tm = MOE_ROWS_STEP
    final = final_gain is not None
    in_specs = [pl.BlockSpec((1, 1, 2 * tm), lambda i: (i, 0, 0), memory_space=pltpu.SMEM),
                pl.BlockSpec((tm, d), lambda i: (i, 0)), pl.BlockSpec((tm, LANES), lambda i: (i, 0)),
                pl.BlockSpec((tm, p.shape[1]), lambda i: (i, 0)), pl.BlockSpec((1, d), lambda i: (0, 0)),
                pl.BlockSpec((d, d), lambda i: (0, 0), pipeline_mode=pl.Buffered(1)),
                pl.BlockSpec((p.shape[1], d), lambda i: (0, 0), pipeline_mode=pl.Buffered(1))]
    args = [pos, x, route, p, ple_gain.reshape(1, d), w_ple_gate, w_ple]
    if final:
        in_specs.append(pl.BlockSpec((1, d), lambda i: (0, 0)))
        args.append(final_gain.reshape(1, d))
    in_specs.append(pl.BlockSpec(memory_space=pl.ANY))
    args.append(ys)
    return pl.pallas_call(
        functools.partial(_combine_body, final=final), grid=(t // tm,), in_specs=in_specs,
        out_specs=pl.BlockSpec((tm, d), lambda i: (i, 0)), out_shape=jax.ShapeDtypeStruct((t, d), F32),
        scratch_shapes=[pltpu.VMEM((2, tm, d), F32), pltpu.SemaphoreType.DMA(())],
        compiler_params=_cparams(1), name="moe_combine_ple")(*args)


def moe_ple(x, ffn_gain, w_router, wg, wu, wd, p, ple_gain, w_ple_gate, w_ple, final_gain=None):
    t = x.shape[0]
    n_rows_pad = 2 * t + N_EXPERTS * MOE_TILE
    route = moe_route(x, ffn_gain, w_router)
    pos, tile_expert, tile_valid = _row_positions(route, n_rows_pad)
    xs = scatter_rows(x, pos, n_rows_pad)
    ys = grouped_swiglu(xs, ffn_gain, wg, wu, wd, tile_expert, tile_valid)
    return combine_ple(x, route, pos, ys, p, ple_gain, w_ple_gate, w_ple, final_gain)


CONV_PAD = 8


def _causal_conv_silu(xf, w_ref, pad_ref, bias=None):
    k_w = w_ref.shape[0]
    s = xf.shape[0]
    pad_ref[0:CONV_PAD, :] = jnp.zeros((CONV_PAD, xf.shape[1]), F32)
    pad_ref[CONV_PAD:CONV_PAD + s, :] = xf
    y = xf * w_ref[k_w - 1:k_w, :]
    for j in range(k_w - 1):
        start = CONV_PAD - (k_w - 1 - j)
        y = y + pad_ref[start:start + s, :] * w_ref[j:j + 1, :]
    if bias is not None:
        y = y + bias
    return _silu(y)


def _chunk_cumsum(x, rows):
    s = 1
    while s < CHUNK:
        x = x + jnp.where(rows % CHUNK >= s, pltpu.roll(x, s, 0), 0.0)
        s *= 2
    return x


def _tri_iotas():
    return lax.broadcasted_iota(jnp.int32, (CHUNK, CHUNK), 0), lax.broadcasted_iota(jnp.int32, (CHUNK, CHUNK), 1)


def _decay_from(g_col, g_row, incl):
    return jnp.where(incl, jnp.exp(jnp.where(incl, g_col - g_row, 0.0)), 0.0)


def _decay_matrix(gm_b, incl):
    return _decay_from(gm_b, gm_b.T, incl)


def _unit_lower_inverses(mats, ri, ci):
    eye = (ri == ci).astype(F32)
    first = (ri - ci == 1) & (ri % 2 == 1)
    ts = [eye - jnp.where(first, a, 0.0) for a in mats]
    s = 2
    while s < CHUNK:
        off_mask = ((ri // s) % 2 == 1) & (ci // s == ri // s - 1)
        tbs = [t.astype(BF16) for t in ts]
        inner = [_dot(jnp.where(off_mask, a, 0.0).astype(BF16), tb).astype(BF16) for a, tb in zip(mats, tbs)]
        outer = [_dot(tb, x) for tb, x in zip(tbs, inner)]
        ts = [t - x for t, x in zip(ts, outer)]
        s *= 2
    return ts


def _dn_body(alog_ref, dtb_ref, q_ref, k_ref, v_ref, z_ref, sm_ref, cwq_ref, cwk_ref, cwv_ref, ng_ref, o_ref,
             qn_scr, kn_scr, vc_scr, gam_scr, beta_scr, p_scr, n_scr, qp_scr, o0_scr, gl_scr, pad_scr, gall_scr, ball_scr):
    s = q_ref.shape[0]
    hd = DN_HEAD_DIM
    n_chunks = s // CHUNK
    lane = lax.broadcasted_iota(jnp.int32, (s, LANES), 1)
    ri, ci = _tri_iotas()
    incl = ri >= ci
    strict = ri > ci

    def l2n(x):
        return x * lax.rsqrt(jnp.sum(x * x, axis=-1, keepdims=True) + RMS_EPS)

    @pl.when(pl.program_id(1) == 0)
    def _():
        rows = lax.broadcasted_iota(jnp.int32, (s, LANES), 0)
        sm = sm_ref[...]
        gall_scr[...] = _chunk_cumsum(-jnp.exp(alog_ref[...]) * _softplus(sm + dtb_ref[...]), rows)
        ball_scr[...] = jax.nn.sigmoid(sm)

    for hh in range(DN_HEADS_PER_STEP):
        h = pl.program_id(1) * DN_HEADS_PER_STEP + hh
        hs = slice(hh * hd, (hh + 1) * hd)
        qn_scr[...] = l2n(_causal_conv_silu(q_ref[:, hs].astype(F32), cwq_ref.at[:, hs], pad_scr)) * hd ** -0.5
        kn_scr[...] = l2n(_causal_conv_silu(k_ref[:, hs].astype(F32), cwk_ref.at[:, hs], pad_scr))
        vc_scr[...] = _causal_conv_silu(v_ref[:, hs].astype(F32), cwv_ref.at[:, hs], pad_scr)
        gam_scr[...] = jnp.broadcast_to(jnp.sum(jnp.where(lane == h, gall_scr[...], 0.0), axis=1, keepdims=True), (s, LANES))
        beta_scr[...] = jnp.broadcast_to(
            jnp.sum(jnp.where(lane == DN_HEADS + h, ball_scr[...], 0.0), axis=1, keepdims=True), (s, LANES))

        def factor_group(cg, carry, h=h):
            us = range(DN_UNROLL)
            sls = [pl.ds(pl.multiple_of((cg * DN_UNROLL + u) * CHUNK, CHUNK), CHUNK) for u in us]
            kc = [kn_scr[sl, :] for sl in sls]
            gm = [gam_scr[sl, :] for sl in sls]
            bt = [beta_scr[sl, :] for sl in sls]
            decay = [_decay_matrix(gm[u], incl) for u in us]
            kb = [kc[u] * bt[u] for u in us]
            kcb = [kc[u].astype(BF16) for u in us]
            kk = [_dot_nt(kb[u].astype(BF16), kcb[u]) for u in us]
            qk = [_dot_nt(qn_scr[sls[u], :].astype(BF16), kcb[u]) for u in us]
            t_inv = _unit_lower_inverses([jnp.where(strict, kk[u] * decay[u], 0.0) for u in us], ri, ci)
            eg = [jnp.exp(gm[u]) for u in us]
            rhs = [jnp.concatenate([kb[u] * eg[u], vc_scr[sls[u], :] * bt[u]], axis=1).astype(BF16) for u in us]
            wu = [_dot(t_inv[u].astype(BF16), rhs[u]).astype(BF16) for u in us]
            g_last = [gm[u][CHUNK - 1:CHUNK, :] for u in us]
            pn = [_dot_tn((kc[u] * jnp.exp(g_last[u] - gm[u])).astype(BF16), wu[u]) for u in us]
            qo = [_dot((qk[u] * decay[u]).astype(BF16), wu[u]) for u in us]
            for u in us:
                p_scr[h, sls[u], :] = pn[u][:, :hd].astype(BF16)
                n_scr[h, sls[u], :] = pn[u][:, hd:]
                qp_scr[h, sls[u], :] = (qn_scr[sls[u], :] * eg[u] - qo[u][:, :hd]).astype(BF16)
                o0_scr[h, sls[u], :] = qo[u][:, hd:].astype(BF16)
                gl_scr[h, pl.ds(pl.multiple_of((cg * DN_UNROLL + u) * 8, 8), 8), :] = jnp.broadcast_to(
                    jnp.exp(g_last[u]), (8, LANES))
            return carry

        lax.fori_loop(0, n_chunks // DN_UNROLL, factor_group, 0)

    @pl.when(pl.program_id(1) == pl.num_programs(1) - 1)
    def _():
        def scan(c, states):
            sl = pl.ds(pl.multiple_of(c * CHUNK, CHUNK), CHUNK)
            sbs = [st.astype(BF16) for st in states]
            outs = [_dot(qp_scr[h, sl, :], sbs[h]) for h in range(DN_HEADS)]
            decs = [_dot(p_scr[h, sl, :], sbs[h]) for h in range(DN_HEADS)]
            new = []
            for h in range(DN_HEADS):
                hs = slice(h * hd, (h + 1) * hd)
                e_last = gl_scr[h, pl.ds(pl.multiple_of(c * 8, 8), 1), :]
                new.append(states[h] * e_last - decs[h] + n_scr[h, sl, :])
                o = outs[h] + o0_scr[h, sl, :]
                on = o * lax.rsqrt(jnp.mean(o * o, axis=-1, keepdims=True) + RMS_EPS) * ng_ref[...]
                o_ref[sl, hs] = (on * _silu(z_ref[sl, hs].astype(F32))).astype(BF16)
            return tuple(new)

        lax.fori_loop(0, n_chunks, scan, tuple(jnp.zeros((hd, hd), F32) for _ in range(DN_HEADS)))


def gated_deltanet(pr_dn, small, conv_w, a_log, dt_bias, norm_g, bsz, s):
    t = bsz * s
    hd = DN_HEAD_DIM
    n_steps = DN_HEADS // DN_HEADS_PER_STEP
    assert (s // CHUNK) % DN_UNROLL == 0

    def col(j0):
        return pl.BlockSpec((s, DN_STEP_W), lambda b, h, j0=j0: (b, j0 + h))

    def cw(j0):
        return pl.BlockSpec((CONV_WIDTH, DN_STEP_W), lambda b, h, j0=j0: (0, j0 + h))

    def per_head(dtype):
        return pltpu.VMEM((DN_HEADS, s, hd), dtype)

    def lanes(v):
        return jnp.zeros((1, LANES), F32).at[0, :DN_HEADS].set(v)

    lane_vec = pl.BlockSpec((1, LANES), lambda b, h: (0, 0))
    return pl.pallas_call(
        _dn_body, grid=(bsz, n_steps),
        in_specs=[lane_vec, lane_vec, col(0), col(n_steps), col(2 * n_steps),
                  pl.BlockSpec((s, DN_WIDTH), lambda b, h: (b, 3), pipeline_mode=pl.Buffered(1)),
                  pl.BlockSpec((s, LANES), lambda b, h: (b, 0)),
                  cw(0), cw(n_steps), cw(2 * n_steps), pl.BlockSpec((1, hd), lambda b, h: (0, 0))],
        out_specs=pl.BlockSpec((s, DN_WIDTH), lambda b, h: (b, 0)),
        out_shape=jax.ShapeDtypeStruct((t, DN_WIDTH), BF16),
        scratch_shapes=[pltpu.VMEM((s, hd), F32)] * 5 + [per_head(BF16), per_head(F32), per_head(BF16), per_head(BF16),
                                                        pltpu.VMEM((DN_HEADS, s // CHUNK * 8, LANES), F32),
                                                        pltpu.VMEM((CONV_PAD + s, hd), F32),
                                                        pltpu.VMEM((s, LANES), F32), pltpu.VMEM((s, LANES), F32)],
        compiler_params=_cparams(2), name="gated_deltanet")(
            lanes(a_log), lanes(dt_bias), pr_dn, pr_dn, pr_dn, pr_dn, small, conv_w, conv_w, conv_w, norm_g.reshape(1, hd))


def _expand_heads(v, g, n_rows):
    lane_head = lax.broadcasted_iota(jnp.int32, (n_rows, SSM_GW), 1) // SSM_HEAD_DIM
    out = jnp.zeros((n_rows, SSM_GW), F32)
    for hh in range(SSM_HPG):
        l0 = DT_LANE0 + g * SSM_HPG + hh
        out = jnp.where(lane_head == hh, v[:, l0:l0 + 1], out)
    return out


def _ssd_body(pr_ref, sm_ref, cw_ref, cb_ref, alog_ref, dtb_ref, d_ref, ng_ref, o_ref,
              x_scr, b_scr, c_scr, dt_scr, acs_scr, st_scr, pad_scr):
    s = pr_ref.shape[0]
    rows = lax.broadcasted_iota(jnp.int32, (s, LANES), 0)
    n_state = SSM_GROUPS * SSM_STATE
    k_w = cw_ref.shape[0]
    nr = SSM_CONV_ROWS

    def conv_rows(i, carry):
        rsl = pl.ds(pl.multiple_of(i * nr, nr), nr)
        for j in range(SSM_XBC // LANES):
            cs = slice(j * LANES, (j + 1) * LANES)
            src = j * LANES if j * LANES < SSM_WIDTH else j * LANES + SSM_WIDTH
            xf = pr_ref[rsl, src:src + LANES].astype(F32)
            pad_scr[0:CONV_PAD, cs] = pad_scr[nr:nr + CONV_PAD, cs]
            pad_scr[CONV_PAD:CONV_PAD + nr, cs] = xf
            y = xf * cw_ref[k_w - 1:k_w, cs] + cb_ref[:, cs]
            for tap in range(k_w - 1):
                start = CONV_PAD - (k_w - 1 - tap)
                y = y + pad_scr[start:start + nr, cs] * cw_ref[tap:tap + 1, cs]
            y = _silu(y)
            if j * LANES < SSM_WIDTH:
                x_scr[rsl, cs] = y
            elif j * LANES < SSM_WIDTH + n_state:
                b_scr[rsl, j * LANES - SSM_WIDTH:(j + 1) * LANES - SSM_WIDTH] = y.astype(BF16)
            else:
                c_scr[rsl, j * LANES - SSM_WIDTH - n_state:(j + 1) * LANES - SSM_WIDTH - n_state] = y.astype(BF16)
        return carry

    pad_scr[nr:nr + CONV_PAD, :] = jnp.zeros((CONV_PAD, SSM_XBC), F32)
    lax.fori_loop(0, s // nr, conv_rows, 0)
    dt = _softplus(sm_ref[...] + dtb_ref[...])
    dt_scr[...] = dt
    acs_scr[...] = _chunk_cumsum(dt * (-jnp.exp(alog_ref[...])), rows)
    st_scr[...] = jnp.zeros_like(st_scr)

    ri, ci = _tri_iotas()
    incl = ri >= ci

    def chunk(c, carry):
        sl = pl.ds(pl.multiple_of(c * CHUNK, CHUNK), CHUNK)
        dt_c, acs_c = dt_scr[sl, :], acs_scr[sl, :]
        acs_last = acs_c[CHUNK - 1:CHUNK, :]
        acs_t = acs_c.T
        groups = range(SSM_GROUPS)
        bc = [b_scr[sl, g * SSM_STATE:(g + 1) * SSM_STATE] for g in groups]
        cc = [c_scr[sl, g * SSM_STATE:(g + 1) * SSM_STATE] for g in groups]
        xg = [x_scr[sl, g * SSM_GW:(g + 1) * SSM_GW] for g in groups]
        xdt = [xg[g] * _expand_heads(dt_c, g, CHUNK) for g in groups]
        acs_e = [_expand_heads(acs_c, g, CHUNK) for g in groups]
        last_e = [_expand_heads(acs_last, g, 1) for g in groups]
        cbm = [_dot_nt(cc[g], bc[g]) for g in groups]
        state = [st_scr[g] for g in groups]
        y_off = [_dot(cc[g], state[g].astype(BF16)) for g in groups]
        st_new = [_dot_tn(bc[g], (xdt[g] * jnp.exp(last_e[g] - acs_e[g])).astype(BF16)) for g in groups]
        y_diag = []
        for g in groups:
            xdt_b = xdt[g].astype(BF16)
            for hh in range(SSM_HPG):
                l0 = DT_LANE0 + g * SSM_HPG + hh
                lm = _decay_from(jnp.broadcast_to(acs_c[:, l0:l0 + 1], (CHUNK, CHUNK)),
                                 jnp.broadcast_to(acs_t[l0:l0 + 1, :], (CHUNK, CHUNK)), incl)
                y_diag.append(_dot((cbm[g] * lm).astype(BF16), xdt_b[:, hh * SSM_HEAD_DIM:(hh + 1) * SSM_HEAD_DIM]))
        for g in groups:
            gs = slice(g * SSM_GW, (g + 1) * SSM_GW)
            st_scr[g] = state[g] * jnp.exp(last_e[g]) + st_new[g]
            y = y_off[g] * jnp.exp(acs_e[g]) + _expand_heads(d_ref[...], g, 1) * xg[g]
            y = y + jnp.concatenate(y_diag[g * SSM_HPG:(g + 1) * SSM_HPG], axis=1)
            y = y * _silu(pr_ref[sl, SSM_WIDTH + g * SSM_GW:SSM_WIDTH + (g + 1) * SSM_GW].astype(F32))
            y = y * lax.rsqrt(jnp.mean(y * y, axis=-1, keepdims=True) + RMS_EPS) * ng_ref[:, gs]
            o_ref[sl, gs] = y.astype(BF16)
        return carry

    lax.fori_loop(0, s // CHUNK, chunk, 0)


def mamba2_ssd(pr_ssm, small, conv_w, conv_b, a_log, dt_bias, d_skip, norm_g, bsz, s):
    t = bsz * s

    def lanes(v):
        return jnp.zeros((1, LANES), F32).at[0, DT_LANE0:DT_LANE0 + SSM_HEADS].set(v)

    def whole(shape):
        return pl.BlockSpec(shape, lambda b: (0, 0))

    n_state = SSM_GROUPS * SSM_STATE
    return pl.pallas_call(
        _ssd_body, grid=(bsz,),
        in_specs=[pl.BlockSpec((s, pr_ssm.shape[1]), lambda b: (b, 0)),
                  pl.BlockSpec((s, LANES), lambda b: (b, 0)),
                  whole((CONV_WIDTH, SSM_XBC)), whole((1, SSM_XBC)), whole((1, LANES)), whole((1, LANES)), whole((1, LANES)),
                  whole((1, SSM_WIDTH))],
        out_specs=pl.BlockSpec((s, SSM_WIDTH), lambda b: (b, 0)),
        out_shape=jax.ShapeDtypeStruct((t, SSM_WIDTH), BF16),
        scratch_shapes=[pltpu.VMEM((s, SSM_WIDTH), F32), pltpu.VMEM((s, n_state), BF16), pltpu.VMEM((s, n_state), BF16),
                        pltpu.VMEM((s, LANES), F32), pltpu.VMEM((s, LANES), F32),
                        pltpu.VMEM((SSM_GROUPS, SSM_STATE, SSM_GW), F32),
                        pltpu.VMEM((CONV_PAD + SSM_CONV_ROWS, SSM_XBC), F32)],
        compiler_params=_cparams(1), name="mamba2_ssd")(
            pr_ssm, small, conv_w, conv_b.reshape(1, -1), lanes(a_log), lanes(dt_bias), lanes(d_skip), norm_g.reshape(1, -1))


def _attn_body(pr_ref, o_ref, qf, kf, vf, og, lg):
    s = pr_ref.shape[0]
    c = ATTN_BLOCK
    e = ATTN_HEAD_DIM
    ii = lax.broadcasted_iota(jnp.int32, (c, c), 0)
    jj = lax.broadcasted_iota(jnp.int32, (c, c), 1)
    lane_head = lax.broadcasted_iota(jnp.int32, (c, LANES), 1) // e
    ones_b = jnp.ones((c, LANES), BF16)
    for gi, (window, dil) in enumerate(DILATION_GROUPS):
        w_sub = window // dil
        nb = s // dil // c
        for j in range(HALVES):
            c0 = gi * ATTN_OUT_WIDTH + j * LANES
            qf[j] = pr_ref[:, c0:c0 + LANES].astype(F32) * e ** -0.5
            kf[j] = pr_ref[:, ATTN_QKV + c0:ATTN_QKV + c0 + LANES].astype(F32)
            vf[j] = pr_ref[:, 2 * ATTN_QKV + c0:2 * ATTN_QKV + c0 + LANES].astype(F32)
        d_cur = ii - jj
        d_prev = d_cur + c
        ok_cur = (d_cur >= 0) & (d_cur <= w_sub)
        ok_prev_static = d_prev <= w_sub

        def blocks(it, carry, gi=gi, dil=dil, nb=nb, d_cur=d_cur, d_prev=d_prev, ok_cur=ok_cur, ok_prev_static=ok_prev_static):
            cur, prev, ok_prev = [], [], []
            for b in range(ATTN_LOCKSTEP):
                idx = it * ATTN_LOCKSTEP + b
                r, n = idx // nb, idx % nb
                cur.append(pl.ds(r + n * (c * dil), c, stride=dil))
                prev.append(pl.ds(r + jnp.maximum(n - 1, 0) * (c * dil), c, stride=dil))
                ok_prev.append(ok_prev_static & (n > 0))
            units = [(b, j, hh) for b in range(ATTN_LOCKSTEP) for j in range(HALVES) for hh in range(HEADS_PER_HALF)]
            vcur = [[vf.at[j][cur[b], :].astype(BF16) for j in range(HALVES)] for b in range(ATTN_LOCKSTEP)]
            vprev = [[vf.at[j][prev[b], :].astype(BF16) for j in range(HALVES)] for b in range(ATTN_LOCKSTEP)]
            s_cur, s_prev = [], []
            for b in range(ATTN_LOCKSTEP):
                for j in range(HALVES):
                    q = qf.at[j][cur[b], :]
                    kc, kp = kf.at[j][cur[b], :].astype(BF16), kf.at[j][prev[b], :].astype(BF16)
                    for hh in range(HEADS_PER_HALF):
                        qh = jnp.where(lane_head == hh, q, 0.0).astype(BF16)
                        s_cur.append(_dot_nt(qh, kc))
                        s_prev.append(_dot_nt(qh, kp))
            probs, maxes = [], []
            for i, (b, j, hh) in enumerate(units):
                head = gi * ATTN_GROUP_HEADS + j * HEADS_PER_HALF + hh
                slope = float(2.0 ** (-ALIBI_MAX_BIAS * (head + 1.0) / ATTN_HEADS)) * dil
                sc = jnp.where(ok_cur, s_cur[i] - slope * d_cur.astype(F32), NEG_BIG)
                sp = jnp.where(ok_prev[b], s_prev[i] - slope * d_prev.astype(F32), NEG_BIG)
                m = jnp.max(jnp.maximum(sc, sp), axis=1, keepdims=True)
                probs.append((jnp.exp(sc - m).astype(BF16), jnp.exp(sp - m).astype(BF16)))
                maxes.append(m)
            nums = [_dot(probs[i][0], vcur[b][j]) + _dot(probs[i][1], vprev[b][j]) for i, (b, j, hh) in enumerate(units)]
            dens = [_dot(probs[i][0], ones_b) + _dot(probs[i][1], ones_b) for i in range(len(units))]
            for b in range(ATTN_LOCKSTEP):
                for j in range(HALVES):
                    i0 = (b * HALVES + j) * HEADS_PER_HALF
                    out, lse = nums[i0] / dens[i0], maxes[i0] + jnp.log(dens[i0])
                    for hh in range(1, HEADS_PER_HALF):
                        out = jnp.where(lane_head == hh, nums[i0 + hh] / dens[i0 + hh], out)
                        lse = jnp.where(lane_head == hh, maxes[i0 + hh] + jnp.log(dens[i0 + hh]), lse)
                    og.at[gi, j][cur[b], :] = out
                    lg.at[gi, j][cur[b], :] = lse
            return carry

        lax.fori_loop(0, dil * nb // ATTN_LOCKSTEP, blocks, 0)
    for j in range(HALVES):
        m = jnp.maximum(jnp.maximum(lg[0, j], lg[1, j]), lg[2, j])
        num = jnp.zeros((s, LANES), F32)
        den = jnp.zeros((s, LANES), F32)
        for gi in range(N_DIL):
            w = jnp.exp(lg[gi, j] - m)
            num = num + w * og[gi, j]
            den = den + w
        o_ref[:, j * LANES:(j + 1) * LANES] = (num / den).astype(BF16)


def dilated_attention(pr_attn, bsz, s):
    t = bsz * s
    assert N_DIL == 3
    for window, dil in DILATION_GROUPS:
        assert s % (dil * ATTN_BLOCK) == 0 and window // dil <= ATTN_BLOCK
    return pl.pallas_call(
        _attn_body, grid=(bsz,),
        in_specs=[pl.BlockSpec((s, pr_attn.shape[1]), lambda b: (b, 0))],
        out_specs=pl.BlockSpec((s, ATTN_OUT_WIDTH), lambda b: (b, 0)),
        out_shape=jax.ShapeDtypeStruct((t, ATTN_OUT_WIDTH), BF16),
        scratch_shapes=[pltpu.VMEM((HALVES, s, LANES), F32)] * 3 + [pltpu.VMEM((N_DIL, HALVES, s, LANES), F32)] * 2,
        compiler_params=_cparams(1), name="dilated_attention")(pr_attn)


def _split_w_in(w_in):
    cuts = np.cumsum(IN_SPLIT_SIZES)[:-1].tolist()
    dn_qkv, dn_z, dn_a, dn_b, ssm_xbc, ssm_z, ssm_dt, attn_qkv, gate = jnp.split(w_in, cuts, axis=-1)
    w_dn = jnp.concatenate([dn_qkv, dn_z], axis=1).astype(BF16)
    w_ssm = jnp.concatenate([ssm_xbc[:, :SSM_WIDTH], ssm_z, ssm_xbc[:, SSM_WIDTH:]], axis=1).astype(BF16)
    small = jnp.concatenate([dn_a, dn_b, ssm_dt], axis=1)
    w_small = jnp.pad(small, ((0, 0), (0, LANES - small.shape[1]))).astype(BF16)
    return w_dn, w_ssm, attn_qkv.astype(BF16), gate.astype(BF16), w_small


def kernel(x, p, mix_norm, w_in, dn_conv, dn_a_log, dn_dt_bias, dn_norm, ssm_conv, ssm_conv_b, ssm_a_log, ssm_dt_bias, ssm_d, ssm_norm, w_br_dn, w_br_ssm, w_br_attn, w_out, ffn_norm, w_ff_gate, w_ff_up, w_ff_down, w_router, w_moe_gate, w_moe_up, w_moe_down, ple_norm, w_ple, w_ple_gate, final_norm):
    bsz, s, d = x.shape
    t = bsz * s
    depth = w_in.shape[0]
    x = x.reshape(t, d)
    for i in range(depth):
        w_dn, w_ssm, w_attn, w_gate, w_small = _split_w_in(w_in[i])
        g = mix_norm[i]
        pr_dn = norm_matmul(x, g, w_dn, 512)
        pr_ssm, small = norm_matmul(x, g, w_ssm, 512, w_small=w_small)
        pr_attn = norm_matmul(x, g, w_attn, 768)
        gates = norm_matmul(x, g, w_gate, 512)

        y_dn = gated_deltanet(pr_dn, small, dn_conv[i], dn_a_log[i], dn_dt_bias[i], dn_norm[i], bsz, s)
        y_ssm = mamba2_ssd(pr_ssm, small, ssm_conv[i], ssm_conv_b[i], ssm_a_log[i], ssm_dt_bias[i], ssm_d[i], ssm_norm[i], bsz, s)
        y_attn = dilated_attention(pr_attn, bsz, s)

        x = merge_out(x, y_dn, y_ssm, y_attn, gates,
                      w_br_dn[i].astype(BF16), w_br_ssm[i].astype(BF16), w_br_attn[i].astype(BF16), w_out[i].astype(BF16))

        j = i // 2
        fin = final_norm if i == depth - 1 else None
        common = (p[i].reshape(t, -1), ple_norm[i], w_ple_gate[i].astype(BF16), w_ple[i].astype(BF16))
        if i % 2 == 0:
            x = ffn_ple(x, ffn_norm[i], w_ff_gate[j][None].astype(BF16), w_ff_up[j][None].astype(BF16),
                        w_ff_down[j][None].astype(BF16), *common, tm=1024, sub=256, final_gain=fin)
        else:
            wr = jnp.pad(w_router[j], ((0, 0), (0, LANES - N_EXPERTS)))
            x = moe_ple(x, ffn_norm[i], wr, w_moe_gate[j].astype(BF16), w_moe_up[j].astype(BF16), w_moe_down[j].astype(BF16),
                        *common, final_gain=fin)
    return x.reshape(bsz, s, d)
```

```python
import functools

import numpy as np
import jax
import jax.numpy as jnp
from jax import lax
from jax.experimental import pallas as pl
from jax.experimental.pallas import tpu as pltpu

F32 = jnp.float32
BF16 = jnp.bfloat16

D_MODEL = 1024
PLE_DIM = 256
CONV_WIDTH = 4
RMS_EPS = 1e-6
DN_HEADS = 6
DN_HEAD_DIM = 128
DN_WIDTH = DN_HEADS * DN_HEAD_DIM
SSM_HEADS = 12
SSM_HEAD_DIM = 64
SSM_WIDTH = SSM_HEADS * SSM_HEAD_DIM
SSM_GROUPS = 2
SSM_STATE = 128
SSM_XBC = SSM_WIDTH + 2 * SSM_GROUPS * SSM_STATE
SSM_GW = SSM_WIDTH // SSM_GROUPS
SSM_HPG = SSM_HEADS // SSM_GROUPS
ATTN_HEADS = 12
ATTN_HEAD_DIM = 64
ATTN_QKV = ATTN_HEADS * ATTN_HEAD_DIM
DILATION_GROUPS = ((128, 1), (512, 4), (2048, 16))
N_DIL = len(DILATION_GROUPS)
ATTN_GROUP_HEADS = ATTN_HEADS // N_DIL
ATTN_OUT_WIDTH = ATTN_GROUP_HEADS * ATTN_HEAD_DIM
ATTN_BLOCK = 128
ALIBI_MAX_BIAS = 8.0
N_BRANCHES = 3
FFN_DIM = 2816
N_EXPERTS = 8
EXPERT_DIM = 3584
IN_SPLIT_SIZES = (3 * DN_WIDTH, DN_WIDTH, DN_HEADS, DN_HEADS, SSM_XBC, SSM_WIDTH, SSM_HEADS, 3 * ATTN_QKV, N_BRANCHES * D_MODEL)

LANES = 128
HALVES = ATTN_OUT_WIDTH // LANES
HEADS_PER_HALF = LANES // ATTN_HEAD_DIM
DT_LANE0 = 2 * DN_HEADS
CHUNK = 128
DN_HEADS_PER_STEP = 2
DN_UNROLL = 16
DN_STEP_W = DN_HEADS_PER_STEP * DN_HEAD_DIM
SSM_CONV_ROWS = 256
ATTN_LOCKSTEP = 4
NEG_BIG = -1e30
VMEM_LIMIT_BYTES = 56 * 1024 * 1024


def _cparams(n_axes):
    return pltpu.CompilerParams(dimension_semantics=("arbitrary",) * n_axes, vmem_limit_bytes=VMEM_LIMIT_BYTES)


def _rms(x, g):
    return x * lax.rsqrt(jnp.mean(x * x, axis=-1, keepdims=True) + RMS_EPS) * g


def _dot(a, b):
    return jnp.dot(a, b, preferred_element_type=F32)


def _dot_nt(a, b):
    return lax.dot_general(a, b, (((1,), (1,)), ((), ())), preferred_element_type=F32)


def _dot_tn(a, b):
    return lax.dot_general(a, b, (((0,), (0,)), ((), ())), preferred_element_type=F32)


def _silu(x):
    return x * jax.nn.sigmoid(x)


def _softplus(x):
    return jnp.maximum(x, 0.0) + jnp.log1p(jnp.exp(-jnp.abs(x)))


def _norm_matmul_body(x_ref, g_ref, w_ref, *rest, chunk, with_small):
    if with_small:
        ws_ref, o_ref, os_ref = rest
    else:
        (o_ref,) = rest
    hn = _rms(x_ref[...], g_ref[...]).astype(BF16)
    for c in range(w_ref.shape[1] // chunk):
        sl = slice(c * chunk, (c + 1) * chunk)
        o_ref[:, sl] = _dot(hn, w_ref[:, sl]).astype(BF16)
    if with_small:
        os_ref[...] = _dot(hn, ws_ref[...])


def norm_matmul(x, gain, w, chunk, w_small=None, tm=1024):
    t, d = x.shape
    n = w.shape[1]
    with_small = w_small is not None
    in_specs = [pl.BlockSpec((tm, d), lambda i: (i, 0)),
                pl.BlockSpec((1, d), lambda i: (0, 0)),
                pl.BlockSpec((d, n), lambda i: (0, 0), pipeline_mode=pl.Buffered(1))]
    out_specs = [pl.BlockSpec((tm, n), lambda i: (i, 0))]
    out_shape = [jax.ShapeDtypeStruct((t, n), BF16)]
    args = [x, gain.reshape(1, d), w]
    if with_small:
        in_specs.append(pl.BlockSpec((d, LANES), lambda i: (0, 0)))
        out_specs.append(pl.BlockSpec((tm, LANES), lambda i: (i, 0)))
        out_shape.append(jax.ShapeDtypeStruct((t, LANES), F32))
        args.append(w_small)
    outs = pl.pallas_call(
        functools.partial(_norm_matmul_body, chunk=chunk, with_small=with_small),
        grid=(t // tm,), in_specs=in_specs, out_specs=out_specs, out_shape=out_shape,
        compiler_params=_cparams(1), name="norm_matmul")(*args)
    return outs if with_small else outs[0]


def _merge_body(x_ref, ydn_ref, yssm_ref, yattn_ref, gate_ref, wdn_ref, wssm_ref, wattn_ref, wout_ref, o_ref):
    d = x_ref.shape[1]

    def sig(j):
        return jax.nn.sigmoid(gate_ref[:, j * d:(j + 1) * d].astype(F32))

    m = sig(0) * _dot(ydn_ref[...], wdn_ref[...])
    m = m + sig(1) * _dot(yssm_ref[...], wssm_ref[...])
    m = m + sig(2) * _dot(yattn_ref[...], wattn_ref[...])
    o_ref[...] = x_ref[...] + _dot(m.astype(BF16), wout_ref[...])


def merge_out(x, y_dn, y_ssm, y_attn, gates, w_dn, w_ssm, w_attn, w_out, tm=1024):
    t, d = x.shape

    def rows(width):
        return pl.BlockSpec((tm, width), lambda i: (i, 0))

    def whole(a):
        return pl.BlockSpec(a.shape, lambda i: (0, 0), pipeline_mode=pl.Buffered(1))

    return pl.pallas_call(
        _merge_body, grid=(t // tm,),
        in_specs=[rows(d), rows(y_dn.shape[1]), rows(y_ssm.shape[1]), rows(y_attn.shape[1]), rows(gates.shape[1]),
                  whole(w_dn), whole(w_ssm), whole(w_attn), whole(w_out)],
        out_specs=rows(d), out_shape=jax.ShapeDtypeStruct((t, d), F32),
        compiler_params=_cparams(1), name="merge_out")(x, y_dn, y_ssm, y_attn, gates, w_dn, w_ssm, w_attn, w_out)


def _swiglu(hn, wg_ref, wu_ref, wd_ref, sub):
    y = None
    for c in range(wg_ref.shape[2] // sub):
        sl = slice(c * sub, (c + 1) * sub)
        a = _dot(hn, wg_ref[0, :, sl])
        u = _dot(hn, wu_ref[0, :, sl])
        yc = _dot((_silu(a) * u).astype(BF16), wd_ref[0, sl, :])
        y = yc if y is None else y + yc
    return y


def _ple_epilogue(x1, p_ref, pg_ref, wpg_ref, wp_ref, fg_ref):
    gate = jax.nn.sigmoid(_dot(_rms(x1, pg_ref[...]).astype(BF16), wpg_ref[...]))
    x2 = x1 + _dot(p_ref[...].astype(BF16), wp_ref[...]) * gate
    return x2 if fg_ref is None else _rms(x2, fg_ref[...])


def _ffn_body(*refs, sub, final):
    it = iter(refs)
    x_ref, g_ref, wg_ref, wu_ref, wd_ref, p_ref, pg_ref, wpg_ref, wp_ref = (next(it) for _ in range(9))
    fg_ref = next(it) if final else None
    o_ref = next(it)
    x = x_ref[...]
    y = _swiglu(_rms(x, g_ref[...]).astype(BF16), wg_ref, wu_ref, wd_ref, sub)
    o_ref[...] = _ple_epilogue(x + y, p_ref, pg_ref, wpg_ref, wp_ref, fg_ref)


def ffn_ple(x, ffn_gain, wg, wu, wd, p, ple_gain, w_ple_gate, w_ple, *, tm, sub, final_gain=None):
    t, d = x.shape
    final = final_gain is not None

    def whole(a):
        return pl.BlockSpec(a.shape, lambda i: (0,) * a.ndim, pipeline_mode=pl.Buffered(1))

    in_specs = [pl.BlockSpec((tm, d), lambda i: (i, 0)), pl.BlockSpec((1, d), lambda i: (0, 0)),
                whole(wg), whole(wu), whole(wd),
                pl.BlockSpec((tm, p.shape[1]), lambda i: (i, 0)), pl.BlockSpec((1, d), lambda i: (0, 0)),
                whole(w_ple_gate), whole(w_ple)]
    args = [x, ffn_gain.reshape(1, d), wg, wu, wd, p, ple_gain.reshape(1, d), w_ple_gate, w_ple]
    if final:
        in_specs.append(pl.BlockSpec((1, d), lambda i: (0, 0)))
        args.append(final_gain.reshape(1, d))
    return pl.pallas_call(
        functools.partial(_ffn_body, sub=sub, final=final), grid=(t // tm,), in_specs=in_specs,
        out_specs=pl.BlockSpec((tm, d), lambda i: (i, 0)), out_shape=jax.ShapeDtypeStruct((t, d), F32),
        compiler_params=_cparams(1), name="ffn_ple")(*args)


ROUTE_I1, ROUTE_I2, ROUTE_W1, ROUTE_W2 = 8, 9, 10, 11
MOE_TILE = 512
MOE_ROWS_STEP = 512


def _route_body(x_ref, g_ref, wr_ref, o_ref):
    hn = _rms(x_ref[...], g_ref[...])
    w = wr_ref[...]
    hn_hi, w_hi = hn.astype(BF16), w.astype(BF16)
    hn_lo, w_lo = (hn - hn_hi.astype(F32)).astype(BF16), (w - w_hi.astype(F32)).astype(BF16)
    logits = _dot(hn_hi, w_hi) + (_dot(hn_hi, w_lo) + _dot(hn_lo, w_hi))
    lane = lax.broadcasted_iota(jnp.int32, logits.shape, 1)
    lg = jnp.where(lane < N_EXPERTS, logits, NEG_BIG)
    m1 = jnp.max(lg, axis=1, keepdims=True)
    i1 = jnp.min(jnp.where(lg == m1, lane, LANES), axis=1, keepdims=True)
    lg2 = jnp.where(lane == i1, NEG_BIG, lg)
    m2 = jnp.max(lg2, axis=1, keepdims=True)
    i2 = jnp.min(jnp.where(lg2 == m2, lane, LANES), axis=1, keepdims=True)
    e2 = jnp.exp(m2 - m1)
    w1 = 1.0 / (1.0 + e2)
    rec = jnp.where(lane == ROUTE_I1, i1.astype(F32), 0.0) + jnp.where(lane == ROUTE_I2, i2.astype(F32), 0.0)
    o_ref[...] = rec + jnp.where(lane == ROUTE_W1, w1, 0.0) + jnp.where(lane == ROUTE_W2, e2 * w1, 0.0)


def moe_route(x, gain, w_router, tm=1024):
    t, d = x.shape
    return pl.pallas_call(
        _route_body, grid=(t // tm,),
        in_specs=[pl.BlockSpec((tm, d), lambda i: (i, 0)), pl.BlockSpec((1, d), lambda i: (0, 0)),
                  pl.BlockSpec((d, LANES), lambda i: (0, 0))],
        out_specs=pl.BlockSpec((tm, LANES), lambda i: (i, 0)), out_shape=jax.ShapeDtypeStruct((t, LANES), F32),
        compiler_params=_cparams(1), name="moe_route")(x, gain.reshape(1, d), w_router)


def _row_positions(route, n_rows_pad):
    t = route.shape[0]
    ids = route[:, ROUTE_I1:ROUTE_I2 + 1].astype(jnp.int32).reshape(-1)
    onehot = (ids[:, None] == jnp.arange(N_EXPERTS, dtype=jnp.int32)[None, :]).astype(jnp.int32)
    csum = jnp.cumsum(onehot, axis=0)
    rank = jnp.sum((csum - onehot) * onehot, axis=1)
    counts = csum[-1]
    sizes = (counts + MOE_TILE - 1) // MOE_TILE * MOE_TILE
    ends = jnp.cumsum(sizes)
    pos = (ends - sizes)[ids] + rank
    tile_start = jnp.arange(n_rows_pad // MOE_TILE, dtype=jnp.int32) * MOE_TILE
    tile_valid = (tile_start < ends[-1]).astype(jnp.int32)
    tile_expert = jnp.minimum(jnp.sum((tile_start[:, None] >= ends[None, :]).astype(jnp.int32), axis=1), N_EXPERTS - 1)
    last_expert = jnp.max(jnp.where(sizes > 0, jnp.arange(N_EXPERTS, dtype=jnp.int32), 0))
    tile_expert = jnp.where(tile_valid == 1, tile_expert, last_expert)
    gap_start = jnp.concatenate([ends - sizes + counts, ends[-1:]])
    gap_size = jnp.concatenate([sizes - counts, n_rows_pad - ends[-1:]])
    gap_end = jnp.cumsum(gap_size)
    k = jnp.arange(n_rows_pad - 2 * t, dtype=jnp.int32)
    seg = jnp.sum((k[:, None] >= gap_end[None, :]).astype(jnp.int32), axis=1)
    pos_gap = gap_start[seg] + k - (gap_end - gap_size)[seg]
    pos_all = jnp.concatenate([pos, pos_gap]).astype(jnp.int32)
    return pos_all.reshape(-1, 1, 2 * MOE_ROWS_STEP), tile_expert, tile_valid


ROW_DMA_UNROLL = 8


def _row_copy_burst(n_rows, row_copy):
    def start(j, c):
        row_copy(j, 0).start(priority=0)
        row_copy(j, 1).start(priority=1)
        return c

    def wait(j, c):
        row_copy(j, 0).wait()
        row_copy(j, 1).wait()
        return c

    lax.fori_loop(0, n_rows, start, 0, unroll=ROW_DMA_UNROLL)
    lax.fori_loop(0, n_rows, wait, 0, unroll=ROW_DMA_UNROLL)


def _scatter_rows_body(pos_ref, x_ref, xs_ref, zero_scr, sem, *, n_token_steps):
    def burst(src_ref):
        def row_copy(j, k):
            return pltpu.make_async_copy(src_ref.at[pl.ds(j, 1), :], xs_ref.at[pl.ds(pos_ref[0, 0, 2 * j + k], 1), :], sem)

        _row_copy_burst(x_ref.shape[0], row_copy)

    @pl.when(pl.program_id(0) < n_token_steps)
    def _():
        burst(x_ref)

    @pl.when(pl.program_id(0) >= n_token_steps)
    def _():
        zero_scr[...] = jnp.zeros_like(zero_scr)
        burst(zero_scr)


def scatter_rows(x, pos_all, n_rows_pad):
    t, d = x.shape
    tm = MOE_ROWS_STEP
    n_token_steps = t // tm
    assert pos_all.shape[0] * 2 * tm == n_rows_pad
    return pl.pallas_call(
        functools.partial(_scatter_rows_body, n_token_steps=n_token_steps), grid=(pos_all.shape[0],),
        in_specs=[pl.BlockSpec((1, 1, 2 * tm), lambda i: (i, 0, 0), memory_space=pltpu.SMEM),
                  pl.BlockSpec((tm, d), lambda i: (jnp.minimum(i, n_token_steps - 1), 0))],
        out_specs=pl.BlockSpec(memory_space=pl.ANY), out_shape=jax.ShapeDtypeStruct((n_rows_pad, d), F32),
        scratch_shapes=[pltpu.VMEM((tm, d), F32), pltpu.SemaphoreType.DMA(())],
        compiler_params=_cparams(1), name="moe_scatter_rows")(pos_all, x)


def _expert_body(te_ref, tv_ref, xs_ref, g_ref, wg_ref, wu_ref, wd_ref, o_ref, *, sub):
    del te_ref

    @pl.when(tv_ref[pl.program_id(0)] == 1)
    def _():
        o_ref[...] = _swiglu(_rms(xs_ref[...], g_ref[...]).astype(BF16), wg_ref, wu_ref, wd_ref, sub)

    @pl.when(tv_ref[pl.program_id(0)] == 0)
    def _():
        o_ref[...] = jnp.zeros_like(o_ref)


def grouped_swiglu(xs, gain, wg, wu, wd, tile_expert, tile_valid, sub=512):
    n_rows, d = xs.shape
    f_dim = wg.shape[2]
    grid_spec = pltpu.PrefetchScalarGridSpec(
        num_scalar_prefetch=2, grid=(n_rows // MOE_TILE,),
        in_specs=[pl.BlockSpec((MOE_TILE, d), lambda i, te, tv: (i, 0)),
                  pl.BlockSpec((1, d), lambda i, te, tv: (0, 0)),
                  pl.BlockSpec((1, d, f_dim), lambda i, te, tv: (te[i], 0, 0), pipeline_mode=pl.Buffered(1)),
                  pl.BlockSpec((1, d, f_dim), lambda i, te, tv: (te[i], 0, 0), pipeline_mode=pl.Buffered(1)),
                  pl.BlockSpec((1, f_dim, d), lambda i, te, tv: (te[i], 0, 0), pipeline_mode=pl.Buffered(1))],
        out_specs=pl.BlockSpec((MOE_TILE, d), lambda i, te, tv: (i, 0)))
    return pl.pallas_call(
        functools.partial(_expert_body, sub=sub), grid_spec=grid_spec,
        out_shape=jax.ShapeDtypeStruct((n_rows, d), F32),
        compiler_params=_cparams(1), name="moe_grouped_swiglu")(tile_expert, tile_valid, xs, gain.reshape(1, d), wg, wu, wd)


def _combine_body(*refs, final):
    it = iter(refs)
    pos_ref, pos_next_ref, x_ref, route_ref, p_ref, pg_ref, wpg_ref, wp_ref = (next(it) for _ in range(8))
    fg_ref = next(it) if final else None
    ys_ref, o_ref, buf, sems = (next(it) for _ in range(4))
    i = pl.program_id(0)
    n_rows = x_ref.shape[0]
    slot, next_slot = i % 2, (i + 1) % 2

    def row_copy(p_ref_, s, j, k):
        return pltpu.make_async_copy(ys_ref.at[pl.ds(p_ref_[0, 0, 2 * j + k], 1), :], buf.at[s, k, pl.ds(j, 1), :], sems.at[s])

    def wait_rows(p_ref_, s):
        def wait(j, c):
            row_copy(p_ref_, s, j, 0).wait()
            row_copy(p_ref_, s, j, 1).wait()
            return c

        lax.fori_loop(0, n_rows, wait, 0, unroll=ROW_DMA_UNROLL)

    @pl.when(i == 0)
    def _():
        def start(j, c):
            row_copy(pos_ref, 0, j, 0).start(priority=0)
            row_copy(pos_ref, 0, j, 1).start(priority=1)
            return c

        lax.fori_loop(0, n_rows, start, 0, unroll=ROW_DMA_UNROLL)

    wait_rows(pos_ref, slot)
    for j in range(n_rows):
        row_copy(pos_next_ref, next_slot, j, 0).start(priority=0)
        row_copy(pos_next_ref, next_slot, j, 1).start(priority=1)
    route = route_ref[...]
    moe = route[:, ROUTE_W1:ROUTE_W1 + 1] * buf[slot, 0] + route[:, ROUTE_W2:ROUTE_W2 + 1] * buf[slot, 1]
    o_ref[...] = _ple_epilogue(x_ref[...] + moe, p_ref, pg_ref, wpg_ref, wp_ref, fg_ref)

    @pl.when(i == pl.num_programs(0) - 1)
    def _():
        wait_rows(pos_next_ref, next_slot)


def combine_ple(x, route, pos, ys, p, ple_gain, w_ple_gate, w_ple, final_gain=None):
    t, d = x.shape
    tm = MOE_ROWS_STEP
    final = final_gain is not None
    n_steps = t // tm
    in_specs = [pl.BlockSpec((1, 1, 2 * tm), lambda i: (i, 0, 0), memory_space=pltpu.SMEM),
                pl.BlockSpec((1, 1, 2 * tm), lambda i: (jnp.minimum(i + 1, n_steps - 1), 0, 0), memory_space=pltpu.SMEM),
                pl.BlockSpec((tm, d), lambda i: (i, 0)), pl.BlockSpec((tm, LANES), lambda i: (i, 0)),
                pl.BlockSpec((tm, p.shape[1]), lambda i: (i, 0)), pl.BlockSpec((1, d), lambda i: (0, 0)),
                pl.BlockSpec((d, d), lambda i: (0, 0), pipeline_mode=pl.Buffered(1)),
                pl.BlockSpec((p.shape[1], d), lambda i: (0, 0), pipeline_mode=pl.Buffered(1))]
    args = [pos, pos, x, route, p, ple_gain.reshape(1, d), w_ple_gate, w_ple]
    if final:
        in_specs.append(pl.BlockSpec((1, d), lambda i: (0, 0)))
        args.append(final_gain.reshape(1, d))
    in_specs.append(pl.BlockSpec(memory_space=pl.ANY))
    args.append(ys)
    return pl.pallas_call(
        functools.partial(_combine_body, final=final), grid=(n_steps,), in_specs=in_specs,
        out_specs=pl.BlockSpec((tm, d), lambda i: (i, 0)), out_shape=jax.ShapeDtypeStruct((t, d), F32),
        scratch_shapes=[pltpu.VMEM((2, 2, tm, d), F32), pltpu.SemaphoreType.DMA((2,))],
        compiler_params=_cparams(1), name="moe_combine_ple")(*args)


def moe_ple(x, ffn_gain, w_router, wg, wu, wd, p, ple_gain, w_ple_gate, w_ple, final_gain=None):
    t = x.shape[0]
    n_rows_pad = 2 * t + N_EXPERTS * MOE_TILE
    route = moe_route(x, ffn_gain, w_router)
    pos, tile_expert, tile_valid = _row_positions(route, n_rows_pad)
    xs = scatter_rows(x, pos, n_rows_pad)
    ys = grouped_swiglu(xs, ffn_gain, wg, wu, wd, tile_expert, tile_valid)
    return combine_ple(x, route, pos, ys, p, ple_gain, w_ple_gate, w_ple, final_gain)


CONV_PAD = 8


def _causal_conv_silu(xf, w_ref, pad_ref, bias=None):
    k_w = w_ref.shape[0]
    s = xf.shape[0]
    pad_ref[0:CONV_PAD, :] = jnp.zeros((CONV_PAD, xf.shape[1]), F32)
    pad_ref[CONV_PAD:CONV_PAD + s, :] = xf
    y = xf * w_ref[k_w - 1:k_w, :]
    for j in range(k_w - 1):
        start = CONV_PAD - (k_w - 1 - j)
        y = y + pad_ref[start:start + s, :] * w_ref[j:j + 1, :]
    if bias is not None:
        y = y + bias
    return _silu(y)


def _chunk_cumsum(x, rows):
    s = 1
    while s < CHUNK:
        x = x + jnp.where(rows % CHUNK >= s, pltpu.roll(x, s, 0), 0.0)
        s *= 2
    return x


def _tri_iotas():
    return lax.broadcasted_iota(jnp.int32, (CHUNK, CHUNK), 0), lax.broadcasted_iota(jnp.int32, (CHUNK, CHUNK), 1)


def _decay_from(g_col, g_row, incl):
    return jnp.where(incl, jnp.exp(jnp.where(incl, g_col - g_row, 0.0)), 0.0)


def _decay_matrix(gm_b, incl):
    return _decay_from(gm_b, gm_b.T, incl)


def _unit_lower_inverses(mats, ri, ci):
    eye = (ri == ci).astype(F32)
    first = (ri - ci == 1) & (ri % 2 == 1)
    ts = [eye - jnp.where(first, a, 0.0) for a in mats]
    s = 2
    while s < CHUNK:
        off_mask = ((ri // s) % 2 == 1) & (ci // s == ri // s - 1)
        tbs = [t.astype(BF16) for t in ts]
        inner = [_dot(jnp.where(off_mask, a, 0.0).astype(BF16), tb).astype(BF16) for a, tb in zip(mats, tbs)]
        outer = [_dot(tb, x) for tb, x in zip(tbs, inner)]
        ts = [t - x for t, x in zip(ts, outer)]
        s *= 2
    return ts


def _dn_body(alog_ref, dtb_ref, q_ref, k_ref, v_ref, z_ref, sm_ref, cwq_ref, cwk_ref, cwv_ref, ng_ref, o_ref,
             qn_scr, kn_scr, vc_scr, gam_scr, beta_scr, p_scr, n_scr, qp_scr, o0_scr, gl_scr, pad_scr, gall_scr, ball_scr):
    s = q_ref.shape[0]
    hd = DN_HEAD_DIM
    n_chunks = s // CHUNK
    lane = lax.broadcasted_iota(jnp.int32, (s, LANES), 1)
    ri, ci = _tri_iotas()
    incl = ri >= ci
    strict = ri > ci

    def l2n(x):
        return x * lax.rsqrt(jnp.sum(x * x, axis=-1, keepdims=True) + RMS_EPS)

    @pl.when(pl.program_id(1) == 0)
    def _():
        rows = lax.broadcasted_iota(jnp.int32, (s, LANES), 0)
        sm = sm_ref[...]
        gall_scr[...] = _chunk_cumsum(-jnp.exp(alog_ref[...]) * _softplus(sm + dtb_ref[...]), rows)
        ball_scr[...] = jax.nn.sigmoid(sm)

    for hh in range(DN_HEADS_PER_STEP):
        h = pl.program_id(1) * DN_HEADS_PER_STEP + hh
        hs = slice(hh * hd, (hh + 1) * hd)
        qn_scr[...] = l2n(_causal_conv_silu(q_ref[:, hs].astype(F32), cwq_ref.at[:, hs], pad_scr)) * hd ** -0.5
        kn_scr[...] = l2n(_causal_conv_silu(k_ref[:, hs].astype(F32), cwk_ref.at[:, hs], pad_scr))
        vc_scr[...] = _causal_conv_silu(v_ref[:, hs].astype(F32), cwv_ref.at[:, hs], pad_scr)
        gam_scr[...] = jnp.broadcast_to(jnp.sum(jnp.where(lane == h, gall_scr[...], 0.0), axis=1, keepdims=True), (s, LANES))
        beta_scr[...] = jnp.broadcast_to(
            jnp.sum(jnp.where(lane == DN_HEADS + h, ball_scr[...], 0.0), axis=1, keepdims=True), (s, LANES))

        def factor_group(cg, carry, h=h):
            us = range(DN_UNROLL)
            sls = [pl.ds(pl.multiple_of((cg * DN_UNROLL + u) * CHUNK, CHUNK), CHUNK) for u in us]
            kc = [kn_scr[sl, :] for sl in sls]
            gm = [gam_scr[sl, :] for sl in sls]
            bt = [beta_scr[sl, :] for sl in sls]
            decay = [_decay_matrix(gm[u], incl) for u in us]
            kb = [kc[u] * bt[u] for u in us]
            kcb = [kc[u].astype(BF16) for u in us]
            kk = [_dot_nt(kb[u].astype(BF16), kcb[u]) for u in us]
            qk = [_dot_nt(qn_scr[sls[u], :].astype(BF16), kcb[u]) for u in us]
            t_inv = _unit_lower_inverses([jnp.where(strict, kk[u] * decay[u], 0.0) for u in us], ri, ci)
            eg = [jnp.exp(gm[u]) for u in us]
            rhs = [jnp.concatenate([kb[u] * eg[u], vc_scr[sls[u], :] * bt[u]], axis=1).astype(BF16) for u in us]
            wu = [_dot(t_inv[u].astype(BF16), rhs[u]).astype(BF16) for u in us]
            g_last = [gm[u][CHUNK - 1:CHUNK, :] for u in us]
            pn = [_dot_tn((kc[u] * jnp.exp(g_last[u] - gm[u])).astype(BF16), wu[u]) for u in us]
            qo = [_dot((qk[u] * decay[u]).astype(BF16), wu[u]) for u in us]
            for u in us:
                p_scr[h, sls[u], :] = pn[u][:, :hd].astype(BF16)
                n_scr[h, sls[u], :] = pn[u][:, hd:]
                qp_scr[h, sls[u], :] = (qn_scr[sls[u], :] * eg[u] - qo[u][:, :hd]).astype(BF16)
                o0_scr[h, sls[u], :] = qo[u][:, hd:].astype(BF16)
                gl_scr[h, pl.ds(pl.multiple_of((cg * DN_UNROLL + u) * 8, 8), 8), :] = jnp.broadcast_to(
                    jnp.exp(g_last[u]), (8, LANES))
            return carry

        lax.fori_loop(0, n_chunks // DN_UNROLL, factor_group, 0)

    @pl.when(pl.program_id(1) == pl.num_programs(1) - 1)
    def _():
        def scan(c, states):
            sl = pl.ds(pl.multiple_of(c * CHUNK, CHUNK), CHUNK)
            sbs = [st.astype(BF16) for st in states]
            outs = [_dot(qp_scr[h, sl, :], sbs[h]) for h in range(DN_HEADS)]
            decs = [_dot(p_scr[h, sl, :], sbs[h]) for h in range(DN_HEADS)]
            new = []
            for h in range(DN_HEADS):
                hs = slice(h * hd, (h + 1) * hd)
                e_last = gl_scr[h, pl.ds(pl.multiple_of(c * 8, 8), 1), :]
                new.append(states[h] * e_last - decs[h] + n_scr[h, sl, :])
                o = outs[h] + o0_scr[h, sl, :]
                on = o * lax.rsqrt(jnp.mean(o * o, axis=-1, keepdims=True) + RMS_EPS) * ng_ref[...]
                o_ref[sl, hs] = (on * _silu(z_ref[sl, hs].astype(F32))).astype(BF16)
            return tuple(new)

        lax.fori_loop(0, n_chunks, scan, tuple(jnp.zeros((hd, hd), F32) for _ in range(DN_HEADS)))


def gated_deltanet(pr_dn, small, conv_w, a_log, dt_bias, norm_g, bsz, s):
    t = bsz * s
    hd = DN_HEAD_DIM
    n_steps = DN_HEADS // DN_HEADS_PER_STEP
    assert (s // CHUNK) % DN_UNROLL == 0

    def col(j0):
        return pl.BlockSpec((s, DN_STEP_W), lambda b, h, j0=j0: (b, j0 + h))

    def cw(j0):
        return pl.BlockSpec((CONV_WIDTH, DN_STEP_W), lambda b, h, j0=j0: (0, j0 + h))

    def per_head(dtype):
        return pltpu.VMEM((DN_HEADS, s, hd), dtype)

    def lanes(v):
        return jnp.zeros((1, LANES), F32).at[0, :DN_HEADS].set(v)

    lane_vec = pl.BlockSpec((1, LANES), lambda b, h: (0, 0))
    return pl.pallas_call(
        _dn_body, grid=(bsz, n_steps),
        in_specs=[lane_vec, lane_vec, col(0), col(n_steps), col(2 * n_steps),
                  pl.BlockSpec((s, DN_WIDTH), lambda b, h: (b, 3), pipeline_mode=pl.Buffered(1)),
                  pl.BlockSpec((s, LANES), lambda b, h: (b, 0)),
                  cw(0), cw(n_steps), cw(2 * n_steps), pl.BlockSpec((1, hd), lambda b, h: (0, 0))],
        out_specs=pl.BlockSpec((s, DN_WIDTH), lambda b, h: (b, 0)),
        out_shape=jax.ShapeDtypeStruct((t, DN_WIDTH), BF16),
        scratch_shapes=[pltpu.VMEM((s, hd), F32)] * 5 + [per_head(BF16), per_head(F32), per_head(BF16), per_head(BF16),
                                                        pltpu.VMEM((DN_HEADS, s // CHUNK * 8, LANES), F32),
                                                        pltpu.VMEM((CONV_PAD + s, hd), F32),
                                                        pltpu.VMEM((s, LANES), F32), pltpu.VMEM((s, LANES), F32)],
        compiler_params=_cparams(2), name="gated_deltanet")(
            lanes(a_log), lanes(dt_bias), pr_dn, pr_dn, pr_dn, pr_dn, small, conv_w, conv_w, conv_w, norm_g.reshape(1, hd))


def _expand_heads(v, g, n_rows):
    lane_head = lax.broadcasted_iota(jnp.int32, (n_rows, SSM_GW), 1) // SSM_HEAD_DIM
    out = jnp.zeros((n_rows, SSM_GW), F32)
    for hh in range(SSM_HPG):
        l0 = DT_LANE0 + g * SSM_HPG + hh
        out = jnp.where(lane_head == hh, v[:, l0:l0 + 1], out)
    return out


def _ssd_body(pr_ref, sm_ref, cw_ref, cb_ref, alog_ref, dtb_ref, d_ref, ng_ref, o_ref,
              x_scr, b_scr, c_scr, dt_scr, acs_scr, st_scr, pad_scr):
    s = pr_ref.shape[0]
    rows = lax.broadcasted_iota(jnp.int32, (s, LANES), 0)
    n_state = SSM_GROUPS * SSM_STATE
    k_w = cw_ref.shape[0]
    nr = SSM_CONV_ROWS

    def conv_rows(i, carry):
        rsl = pl.ds(pl.multiple_of(i * nr, nr), nr)
        for j in range(SSM_XBC // LANES):
            cs = slice(j * LANES, (j + 1) * LANES)
            src = j * LANES if j * LANES < SSM_WIDTH else j * LANES + SSM_WIDTH
            xf = pr_ref[rsl, src:src + LANES].astype(F32)
            pad_scr[0:CONV_PAD, cs] = pad_scr[nr:nr + CONV_PAD, cs]
            pad_scr[CONV_PAD:CONV_PAD + nr, cs] = xf
            y = xf * cw_ref[k_w - 1:k_w, cs] + cb_ref[:, cs]
            for tap in range(k_w - 1):
                start = CONV_PAD - (k_w - 1 - tap)
                y = y + pad_scr[start:start + nr, cs] * cw_ref[tap:tap + 1, cs]
            y = _silu(y)
            if j * LANES < SSM_WIDTH:
                x_scr[rsl, cs] = y
            elif j * LANES < SSM_WIDTH + n_state:
                b_scr[rsl, j * LANES - SSM_WIDTH:(j + 1) * LANES - SSM_WIDTH] = y.astype(BF16)
            else:
                c_scr[rsl, j * LANES - SSM_WIDTH - n_state:(j + 1) * LANES - SSM_WIDTH - n_state] = y.astype(BF16)
        return carry

    pad_scr[nr:nr + CONV_PAD, :] = jnp.zeros((CONV_PAD, SSM_XBC), F32)
    lax.fori_loop(0, s // nr, conv_rows, 0)
    dt = _softplus(sm_ref[...] + dtb_ref[...])
    dt_scr[...] = dt
    acs_scr[...] = _chunk_cumsum(dt * (-jnp.exp(alog_ref[...])), rows)
    st_scr[...] = jnp.zeros_like(st_scr)

    ri, ci = _tri_iotas()
    incl = ri >= ci

    def chunk(c, carry):
        sl = pl.ds(pl.multiple_of(c * CHUNK, CHUNK), CHUNK)
        dt_c, acs_c = dt_scr[sl, :], acs_scr[sl, :]
        acs_last = acs_c[CHUNK - 1:CHUNK, :]
        acs_t = acs_c.T
        groups = range(SSM_GROUPS)
        bc = [b_scr[sl, g * SSM_STATE:(g + 1) * SSM_STATE] for g in groups]
        cc = [c_scr[sl, g * SSM_STATE:(g + 1) * SSM_STATE] for g in groups]
        xg = [x_scr[sl, g * SSM_GW:(g + 1) * SSM_GW] for g in groups]
        xdt = [xg[g] * _expand_heads(dt_c, g, CHUNK) for g in groups]
        acs_e = [_expand_heads(acs_c, g, CHUNK) for g in groups]
        last_e = [_expand_heads(acs_last, g, 1) for g in groups]
        cbm = [_dot_nt(cc[g], bc[g]) for g in groups]
        state = [st_scr[g] for g in groups]
        y_off = [_dot(cc[g], state[g].astype(BF16)) for g in groups]
        st_new = [_dot_tn(bc[g], (xdt[g] * jnp.exp(last_e[g] - acs_e[g])).astype(BF16)) for g in groups]
        y_diag = []
        for g in groups:
            xdt_b = xdt[g].astype(BF16)
            for hh in range(SSM_HPG):
                l0 = DT_LANE0 + g * SSM_HPG + hh
                lm = _decay_from(jnp.broadcast_to(acs_c[:, l0:l0 + 1], (CHUNK, CHUNK)),
                                 jnp.broadcast_to(acs_t[l0:l0 + 1, :], (CHUNK, CHUNK)), incl)
                y_diag.append(_dot((cbm[g] * lm).astype(BF16), xdt_b[:, hh * SSM_HEAD_DIM:(hh + 1) * SSM_HEAD_DIM]))
        for g in groups:
            gs = slice(g * SSM_GW, (g + 1) * SSM_GW)
            st_scr[g] = state[g] * jnp.exp(last_e[g]) + st_new[g]
            y = y_off[g] * jnp.exp(acs_e[g]) + _expand_heads(d_ref[...], g, 1) * xg[g]
            y = y + jnp.concatenate(y_diag[g * SSM_HPG:(g + 1) * SSM_HPG], axis=1)
            y = y * _silu(pr_ref[sl, SSM_WIDTH + g * SSM_GW:SSM_WIDTH + (g + 1) * SSM_GW].astype(F32))
            y = y * lax.rsqrt(jnp.mean(y * y, axis=-1, keepdims=True) + RMS_EPS) * ng_ref[:, gs]
            o_ref[sl, gs] = y.astype(BF16)
        return carry

    lax.fori_loop(0, s // CHUNK, chunk, 0)


def mamba2_ssd(pr_ssm, small, conv_w, conv_b, a_log, dt_bias, d_skip, norm_g, bsz, s):
    t = bsz * s

    def lanes(v):
        return jnp.zeros((1, LANES), F32).at[0, DT_LANE0:DT_LANE0 + SSM_HEADS].set(v)

    def whole(shape):
        return pl.BlockSpec(shape, lambda b: (0, 0))

    n_state = SSM_GROUPS * SSM_STATE
    return pl.pallas_call(
        _ssd_body, grid=(bsz,),
        in_specs=[pl.BlockSpec((s, pr_ssm.shape[1]), lambda b: (b, 0)),
                  pl.BlockSpec((s, LANES), lambda b: (b, 0)),
                  whole((CONV_WIDTH, SSM_XBC)), whole((1, SSM_XBC)), whole((1, LANES)), whole((1, LANES)), whole((1, LANES)),
                  whole((1, SSM_WIDTH))],
        out_specs=pl.BlockSpec((s, SSM_WIDTH), lambda b: (b, 0)),
        out_shape=jax.ShapeDtypeStruct((t, SSM_WIDTH), BF16),
        scratch_shapes=[pltpu.VMEM((s, SSM_WIDTH), F32), pltpu.VMEM((s, n_state), BF16), pltpu.VMEM((s, n_state), BF16),
                        pltpu.VMEM((s, LANES), F32), pltpu.VMEM((s, LANES), F32),
                        pltpu.VMEM((SSM_GROUPS, SSM_STATE, SSM_GW), F32),
                        pltpu.VMEM((CONV_PAD + SSM_CONV_ROWS, SSM_XBC), F32)],
        compiler_params=_cparams(1), name="mamba2_ssd")(
            pr_ssm, small, conv_w, conv_b.reshape(1, -1), lanes(a_log), lanes(dt_bias), lanes(d_skip), norm_g.reshape(1, -1))


def _attn_body(pr_ref, o_ref, qf, kf, vf, og, lg):
    s = pr_ref.shape[0]
    c = ATTN_BLOCK
    e = ATTN_HEAD_DIM
    ii = lax.broadcasted_iota(jnp.int32, (c, c), 0)
    jj = lax.broadcasted_iota(jnp.int32, (c, c), 1)
    lane_head = lax.broadcasted_iota(jnp.int32, (c, LANES), 1) // e
    for gi, (window, dil) in enumerate(DILATION_GROUPS):
        w_sub = window // dil
        nb = s // dil // c
        for j in range(HALVES):
            c0 = gi * ATTN_OUT_WIDTH + j * LANES
            qf[j] = pr_ref[:, c0:c0 + LANES].astype(F32) * e ** -0.5
            kf[j] = pr_ref[:, ATTN_QKV + c0:ATTN_QKV + c0 + LANES].astype(F32)
            vf[j] = pr_ref[:, 2 * ATTN_QKV + c0:2 * ATTN_QKV + c0 + LANES].astype(F32)
        d_cur = ii - jj
        d_prev = d_cur + c
        ok_cur = (d_cur >= 0) & (d_cur <= w_sub)
        ok_prev_static = d_prev <= w_sub

        def blocks(it, carry, gi=gi, dil=dil, nb=nb, d_cur=d_cur, d_prev=d_prev, ok_cur=ok_cur, ok_prev_static=ok_prev_static):
            cur, prev, ok_prev = [], [], []
            for b in range(ATTN_LOCKSTEP):
                idx = it * ATTN_LOCKSTEP + b
                r, n = idx // nb, idx % nb
                cur.append(pl.ds(r + n * (c * dil), c, stride=dil))
                prev.append(pl.ds(r + jnp.maximum(n - 1, 0) * (c * dil), c, stride=dil))
                ok_prev.append(ok_prev_static & (n > 0))
            units = [(b, j, hh) for b in range(ATTN_LOCKSTEP) for j in range(HALVES) for hh in range(HEADS_PER_HALF)]
            def values(rows_):
                return [[[jnp.where(lane_head == hh, vf.at[j][rows_[b], :], 1.0).astype(BF16) for hh in range(HEADS_PER_HALF)]
                         for j in range(HALVES)] for b in range(ATTN_LOCKSTEP)]

            vcur, vprev = values(cur), values(prev)
            s_cur, s_prev = [], []
            for b in range(ATTN_LOCKSTEP):
                for j in range(HALVES):
                    q = qf.at[j][cur[b], :]
                    kc, kp = kf.at[j][cur[b], :].astype(BF16), kf.at[j][prev[b], :].astype(BF16)
                    for hh in range(HEADS_PER_HALF):
                        qh = jnp.where(lane_head == hh, q, 0.0).astype(BF16)
                        s_cur.append(_dot_nt(qh, kc))
                        s_prev.append(_dot_nt(qh, kp))
            probs, maxes = [], []
            for i, (b, j, hh) in enumerate(units):
                head = gi * ATTN_GROUP_HEADS + j * HEADS_PER_HALF + hh
                slope = float(2.0 ** (-ALIBI_MAX_BIAS * (head + 1.0) / ATTN_HEADS)) * dil
                sc = jnp.where(ok_cur, s_cur[i] - slope * d_cur.astype(F32), NEG_BIG)
                sp = jnp.where(ok_prev[b], s_prev[i] - slope * d_prev.astype(F32), NEG_BIG)
                m = jnp.max(jnp.maximum(sc, sp), axis=1, keepdims=True)
                probs.append((jnp.exp(sc - m).astype(BF16), jnp.exp(sp - m).astype(BF16)))
                maxes.append(m)
            pv = [_dot(probs[i][0], vcur[b][j][hh]) + _dot(probs[i][1], vprev[b][j][hh]) for i, (b, j, hh) in enumerate(units)]
            for b in range(ATTN_LOCKSTEP):
                for j in range(HALVES):
                    i0 = (b * HALVES + j) * HEADS_PER_HALF
                    first = lane_head == 0
                    num = jnp.where(first, pv[i0], pv[i0 + 1])
                    den = pltpu.roll(jnp.where(first, pv[i0 + 1], pv[i0]), e, 1)
                    og.at[gi, j][cur[b], :] = num / den
                    lg.at[gi, j][cur[b], :] = jnp.where(first, maxes[i0], maxes[i0 + 1]) + jnp.log(den)
            return carry

        lax.fori_loop(0, dil * nb // ATTN_LOCKSTEP, blocks, 0)
    for j in range(HALVES):
        m = jnp.maximum(jnp.maximum(lg[0, j], lg[1, j]), lg[2, j])
        num = jnp.zeros((s, LANES), F32)
        den = jnp.zeros((s, LANES), F32)
        for gi in range(N_DIL):
            w = jnp.exp(lg[gi, j] - m)
            num = num + w * og[gi, j]
            den = den + w
        o_ref[:, j * LANES:(j + 1) * LANES] = (num / den).astype(BF16)


def dilated_attention(pr_attn, bsz, s):
    t = bsz * s
    assert N_DIL == 3 and HEADS_PER_HALF == 2
    for window, dil in DILATION_GROUPS:
        assert s % (dil * ATTN_BLOCK) == 0 and window // dil <= ATTN_BLOCK
    return pl.pallas_call(
        _attn_body, grid=(bsz,),
        in_specs=[pl.BlockSpec((s, pr_attn.shape[1]), lambda b: (b, 0))],
        out_specs=pl.BlockSpec((s, ATTN_OUT_WIDTH), lambda b: (b, 0)),
        out_shape=jax.ShapeDtypeStruct((t, ATTN_OUT_WIDTH), BF16),
        scratch_shapes=[pltpu.VMEM((HALVES, s, LANES), F32)] * 3 + [pltpu.VMEM((N_DIL, HALVES, s, LANES), F32)] * 2,
        compiler_params=_cparams(1), name="dilated_attention")(pr_attn)


def _split_w_in(w_in):
    cuts = np.cumsum(IN_SPLIT_SIZES)[:-1].tolist()
    dn_qkv, dn_z, dn_a, dn_b, ssm_xbc, ssm_z, ssm_dt, attn_qkv, gate = jnp.split(w_in, cuts, axis=-1)
    w_dn = jnp.concatenate([dn_qkv, dn_z], axis=1).astype(BF16)
    w_ssm = jnp.concatenate([ssm_xbc[:, :SSM_WIDTH], ssm_z, ssm_xbc[:, SSM_WIDTH:]], axis=1).astype(BF16)
    small = jnp.concatenate([dn_a, dn_b, ssm_dt], axis=1)
    w_small = jnp.pad(small, ((0, 0), (0, LANES - small.shape[1]))).astype(BF16)
    return w_dn, w_ssm, attn_qkv.astype(BF16), gate.astype(BF16), w_small


def kernel(x, p, mix_norm, w_in, dn_conv, dn_a_log, dn_dt_bias, dn_norm, ssm_conv, ssm_conv_b, ssm_a_log, ssm_dt_bias, ssm_d, ssm_norm, w_br_dn, w_br_ssm, w_br_attn, w_out, ffn_norm, w_ff_gate, w_ff_up, w_ff_down, w_router, w_moe_gate, w_moe_up, w_moe_down, ple_norm, w_ple, w_ple_gate, final_norm):
    bsz, s, d = x.shape
    t = bsz * s
    depth = w_in.shape[0]
    x = x.reshape(t, d)
    for i in range(depth):
        w_dn, w_ssm, w_attn, w_gate, w_small = _split_w_in(w_in[i])
        g = mix_norm[i]
        pr_dn = norm_matmul(x, g, w_dn, 512)
        pr_ssm, small = norm_matmul(x, g, w_ssm, 512, w_small=w_small)
        pr_attn = norm_matmul(x, g, w_attn, 768)
        gates = norm_matmul(x, g, w_gate, 512)

        y_dn = gated_deltanet(pr_dn, small, dn_conv[i], dn_a_log[i], dn_dt_bias[i], dn_norm[i], bsz, s)
        y_ssm = mamba2_ssd(pr_ssm, small, ssm_conv[i], ssm_conv_b[i], ssm_a_log[i], ssm_dt_bias[i], ssm_d[i], ssm_norm[i], bsz, s)
        y_attn = dilated_attention(pr_attn, bsz, s)

        x = merge_out(x, y_dn, y_ssm, y_attn, gates,
                      w_br_dn[i].astype(BF16), w_br_ssm[i].astype(BF16), w_br_attn[i].astype(BF16), w_out[i].astype(BF16))

        j = i // 2
        fin = final_norm if i == depth - 1 else None
        common = (p[i].reshape(t, -1), ple_norm[i], w_ple_gate[i].astype(BF16), w_ple[i].astype(BF16))
        if i % 2 == 0:
            x = ffn_ple(x, ffn_norm[i], w_ff_gate[j][None].astype(BF16), w_ff_up[j][None].astype(BF16),
                        w_ff_down[j][None].astype(BF16), *common, tm=1024, sub=256, final_gain=fin)
        else:
            wr = jnp.pad(w_router[j], ((0, 0), (0, LANES - N_EXPERTS)))
            x = moe_ple(x, ffn_norm[i], wr, w_moe_gate[j].astype(BF16), w_moe_up[j].astype(BF16), w_moe_down[j].astype(BF16),
                        *common, final_gain=fin)
    return x.reshape(bsz, s, d)
```

```python
import functools

import numpy as np
import jax
import jax.numpy as jnp
from jax import lax
from jax.experimental import pallas as pl
from jax.experimental.pallas import tpu as pltpu

F32 = jnp.float32
BF16 = jnp.bfloat16

D_MODEL = 1024
PLE_DIM = 256
CONV_WIDTH = 4
RMS_EPS = 1e-6
DN_HEADS = 6
DN_HEAD_DIM = 128
DN_WIDTH = DN_HEADS * DN_HEAD_DIM
SSM_HEADS = 12
SSM_HEAD_DIM = 64
SSM_WIDTH = SSM_HEADS * SSM_HEAD_DIM
SSM_GROUPS = 2
SSM_STATE = 128
SSM_XBC = SSM_WIDTH + 2 * SSM_GROUPS * SSM_STATE
SSM_GW = SSM_WIDTH // SSM_GROUPS
SSM_HPG = SSM_HEADS // SSM_GROUPS
ATTN_HEADS = 12
ATTN_HEAD_DIM = 64
ATTN_QKV = ATTN_HEADS * ATTN_HEAD_DIM
DILATION_GROUPS = ((128, 1), (512, 4), (2048, 16))
N_DIL = len(DILATION_GROUPS)
ATTN_GROUP_HEADS = ATTN_HEADS // N_DIL
ATTN_OUT_WIDTH = ATTN_GROUP_HEADS * ATTN_HEAD_DIM
ATTN_BLOCK = 128
ALIBI_MAX_BIAS = 8.0
N_BRANCHES = 3
FFN_DIM = 2816
N_EXPERTS = 8
EXPERT_DIM = 3584
IN_SPLIT_SIZES = (3 * DN_WIDTH, DN_WIDTH, DN_HEADS, DN_HEADS, SSM_XBC, SSM_WIDTH, SSM_HEADS, 3 * ATTN_QKV, N_BRANCHES * D_MODEL)

LANES = 128
HALVES = ATTN_OUT_WIDTH // LANES
HEADS_PER_HALF = LANES // ATTN_HEAD_DIM
DT_LANE0 = 2 * DN_HEADS
CHUNK = 128
DN_HEADS_PER_STEP = 2
DN_UNROLL = 16
DN_STEP_W = DN_HEADS_PER_STEP * DN_HEAD_DIM
SSM_CONV_ROWS = 256
SSM_LOCKSTEP = 2
ATTN_LOCKSTEP = 4
NEG_BIG = -1e30
VMEM_LIMIT_BYTES = 56 * 1024 * 1024


def _cparams(n_axes):
    return pltpu.CompilerParams(dimension_semantics=("arbitrary",) * n_axes, vmem_limit_bytes=VMEM_LIMIT_BYTES)


def _rms(x, g):
    return x * lax.rsqrt(jnp.mean(x * x, axis=-1, keepdims=True) + RMS_EPS) * g


def _dot(a, b):
    return jnp.dot(a, b, preferred_element_type=F32)


def _dot_nt(a, b):
    return lax.dot_general(a, b, (((1,), (1,)), ((), ())), preferred_element_type=F32)


def _dot_tn(a, b):
    return lax.dot_general(a, b, (((0,), (0,)), ((), ())), preferred_element_type=F32)


def _silu(x):
    return x * jax.nn.sigmoid(x)


def _softplus(x):
    return jnp.maximum(x, 0.0) + jnp.log1p(jnp.exp(-jnp.abs(x)))


def _norm_matmul_body(x_ref, g_ref, w_ref, *rest, chunk, with_small):
    if with_small:
        ws_ref, o_ref, os_ref = rest
    else:
        (o_ref,) = rest
    hn = _rms(x_ref[...], g_ref[...]).astype(BF16)
    for c in range(w_ref.shape[1] // chunk):
        sl = slice(c * chunk, (c + 1) * chunk)
        o_ref[:, sl] = _dot(hn, w_ref[:, sl]).astype(BF16)
    if with_small:
        os_ref[...] = _dot(hn, ws_ref[...])


def norm_matmul(x, gain, w, chunk, w_small=None, tm=1024):
    t, d = x.shape
    n = w.shape[1]
    with_small = w_small is not None
    in_specs = [pl.BlockSpec((tm, d), lambda i: (i, 0)),
                pl.BlockSpec((1, d), lambda i: (0, 0)),
                pl.BlockSpec((d, n), lambda i: (0, 0), pipeline_mode=pl.Buffered(1))]
    out_specs = [pl.BlockSpec((tm, n), lambda i: (i, 0))]
    out_shape = [jax.ShapeDtypeStruct((t, n), BF16)]
    args = [x, gain.reshape(1, d), w]
    if with_small:
        in_specs.append(pl.BlockSpec((d, LANES), lambda i: (0, 0)))
        out_specs.append(pl.BlockSpec((tm, LANES), lambda i: (i, 0)))
        out_shape.append(jax.ShapeDtypeStruct((t, LANES), F32))
        args.append(w_small)
    outs = pl.pallas_call(
        functools.partial(_norm_matmul_body, chunk=chunk, with_small=with_small),
        grid=(t // tm,), in_specs=in_specs, out_specs=out_specs, out_shape=out_shape,
        compiler_params=_cparams(1), name="norm_matmul")(*args)
    return outs if with_small else outs[0]


def _merge_body(x_ref, ydn_ref, yssm_ref, yattn_ref, gate_ref, wdn_ref, wssm_ref, wattn_ref, wout_ref, o_ref):
    d = x_ref.shape[1]

    def sig(j):
        return jax.nn.sigmoid(gate_ref[:, j * d:(j + 1) * d].astype(F32))

    m = sig(0) * _dot(ydn_ref[...], wdn_ref[...])
    m = m + sig(1) * _dot(yssm_ref[...], wssm_ref[...])
    m = m + sig(2) * _dot(yattn_ref[...], wattn_ref[...])
    o_ref[...] = x_ref[...] + _dot(m.astype(BF16), wout_ref[...])


def merge_out(x, y_dn, y_ssm, y_attn, gates, w_dn, w_ssm, w_attn, w_out, tm=1024):
    t, d = x.shape

    def rows(width):
        return pl.BlockSpec((tm, width), lambda i: (i, 0))

    def whole(a):
        return pl.BlockSpec(a.shape, lambda i: (0, 0), pipeline_mode=pl.Buffered(1))

    return pl.pallas_call(
        _merge_body, grid=(t // tm,),
        in_specs=[rows(d), rows(y_dn.shape[1]), rows(y_ssm.shape[1]), rows(y_attn.shape[1]), rows(gates.shape[1]),
                  whole(w_dn), whole(w_ssm), whole(w_attn), whole(w_out)],
        out_specs=rows(d), out_shape=jax.ShapeDtypeStruct((t, d), F32),
        compiler_params=_cparams(1), name="merge_out")(x, y_dn, y_ssm, y_attn, gates, w_dn, w_ssm, w_attn, w_out)


def _swiglu(hn, wg_ref, wu_ref, wd_ref, sub):
    y = None
    for c in range(wg_ref.shape[2] // sub):
        sl = slice(c * sub, (c + 1) * sub)
        a = _dot(hn, wg_ref[0, :, sl])
        u = _dot(hn, wu_ref[0, :, sl])
        yc = _dot((_silu(a) * u).astype(BF16), wd_ref[0, sl, :])
        y = yc if y is None else y + yc
    return y


def _ple_epilogue(x1, p_ref, pg_ref, wpg_ref, wp_ref, fg_ref):
    gate = jax.nn.sigmoid(_dot(_rms(x1, pg_ref[...]).astype(BF16), wpg_ref[...]))
    x2 = x1 + _dot(p_ref[...].astype(BF16), wp_ref[...]) * gate
    return x2 if fg_ref is None else _rms(x2, fg_ref[...])


def _ffn_body(*refs, sub, final):
    it = iter(refs)
    x_ref, g_ref, wg_ref, wu_ref, wd_ref, p_ref, pg_ref, wpg_ref, wp_ref = (next(it) for _ in range(9))
    fg_ref = next(it) if final else None
    o_ref = next(it)
    x = x_ref[...]
    y = _swiglu(_rms(x, g_ref[...]).astype(BF16), wg_ref, wu_ref, wd_ref, sub)
    o_ref[...] = _ple_epilogue(x + y, p_ref, pg_ref, wpg_ref, wp_ref, fg_ref)


def ffn_ple(x, ffn_gain, wg, wu, wd, p, ple_gain, w_ple_gate, w_ple, *, tm, sub, final_gain=None):
    t, d = x.shape
    final = final_gain is not None

    def whole(a):
        return pl.BlockSpec(a.shape, lambda i: (0,) * a.ndim, pipeline_mode=pl.Buffered(1))

    in_specs = [pl.BlockSpec((tm, d), lambda i: (i, 0)), pl.BlockSpec((1, d), lambda i: (0, 0)),
                whole(wg), whole(wu), whole(wd),
                pl.BlockSpec((tm, p.shape[1]), lambda i: (i, 0)), pl.BlockSpec((1, d), lambda i: (0, 0)),
                whole(w_ple_gate), whole(w_ple)]
    args = [x, ffn_gain.reshape(1, d), wg, wu, wd, p, ple_gain.reshape(1, d), w_ple_gate, w_ple]
    if final:
        in_specs.append(pl.BlockSpec((1, d), lambda i: (0, 0)))
        args.append(final_gain.reshape(1, d))
    return pl.pallas_call(
        functools.partial(_ffn_body, sub=sub, final=final), grid=(t // tm,), in_specs=in_specs,
        out_specs=pl.BlockSpec((tm, d), lambda i: (i, 0)), out_shape=jax.ShapeDtypeStruct((t, d), F32),
        compiler_params=_cparams(1), name="ffn_ple")(*args)


ROUTE_I1, ROUTE_I2, ROUTE_W1, ROUTE_W2 = 8, 9, 10, 11
MOE_TILE = 512
MOE_ROWS_STEP = 512


def _route_body(x_ref, g_ref, wr_ref, o_ref):
    hn = _rms(x_ref[...], g_ref[...])
    w = wr_ref[...]
    hn_hi, w_hi = hn.astype(BF16), w.astype(BF16)
    hn_lo, w_lo = (hn - hn_hi.astype(F32)).astype(BF16), (w - w_hi.astype(F32)).astype(BF16)
    logits = _dot(hn_hi, w_hi) + (_dot(hn_hi, w_lo) + _dot(hn_lo, w_hi))
    lane = lax.broadcasted_iota(jnp.int32, logits.shape, 1)
    lg = jnp.where(lane < N_EXPERTS, logits, NEG_BIG)
    m1 = jnp.max(lg, axis=1, keepdims=True)
    i1 = jnp.min(jnp.where(lg == m1, lane, LANES), axis=1, keepdims=True)
    lg2 = jnp.where(lane == i1, NEG_BIG, lg)
    m2 = jnp.max(lg2, axis=1, keepdims=True)
    i2 = jnp.min(jnp.where(lg2 == m2, lane, LANES), axis=1, keepdims=True)
    e2 = jnp.exp(m2 - m1)
    w1 = 1.0 / (1.0 + e2)
    rec = jnp.where(lane == ROUTE_I1, i1.astype(F32), 0.0) + jnp.where(lane == ROUTE_I2, i2.astype(F32), 0.0)
    o_ref[...] = rec + jnp.where(lane == ROUTE_W1, w1, 0.0) + jnp.where(lane == ROUTE_W2, e2 * w1, 0.0)


def moe_route(x, gain, w_router, tm=1024):
    t, d = x.shape
    return pl.pallas_call(
        _route_body, grid=(t // tm,),
        in_specs=[pl.BlockSpec((tm, d), lambda i: (i, 0)), pl.BlockSpec((1, d), lambda i: (0, 0)),
                  pl.BlockSpec((d, LANES), lambda i: (0, 0))],
        out_specs=pl.BlockSpec((tm, LANES), lambda i: (i, 0)), out_shape=jax.ShapeDtypeStruct((t, LANES), F32),
        compiler_params=_cparams(1), name="moe_route")(x, gain.reshape(1, d), w_router)


def _row_positions(route, n_rows_pad):
    t = route.shape[0]
    ids = route[:, ROUTE_I1:ROUTE_I2 + 1].astype(jnp.int32).reshape(-1)
    onehot = (ids[:, None] == jnp.arange(N_EXPERTS, dtype=jnp.int32)[None, :]).astype(jnp.int32)
    csum = jnp.cumsum(onehot, axis=0)
    rank = jnp.sum((csum - onehot) * onehot, axis=1)
    counts = csum[-1]
    sizes = (counts + MOE_TILE - 1) // MOE_TILE * MOE_TILE
    ends = jnp.cumsum(sizes)
    pos = (ends - sizes)[ids] + rank
    tile_start = jnp.arange(n_rows_pad // MOE_TILE, dtype=jnp.int32) * MOE_TILE
    tile_valid = (tile_start < ends[-1]).astype(jnp.int32)
    tile_expert = jnp.minimum(jnp.sum((tile_start[:, None] >= ends[None, :]).astype(jnp.int32), axis=1), N_EXPERTS - 1)
    last_expert = jnp.max(jnp.where(sizes > 0, jnp.arange(N_EXPERTS, dtype=jnp.int32), 0))
    tile_expert = jnp.where(tile_valid == 1, tile_expert, last_expert)
    gap_start = jnp.concatenate([ends - sizes + counts, ends[-1:]])
    gap_size = jnp.concatenate([sizes - counts, n_rows_pad - ends[-1:]])
    gap_end = jnp.cumsum(gap_size)
    k = jnp.arange(n_rows_pad - 2 * t, dtype=jnp.int32)
    seg = jnp.sum((k[:, None] >= gap_end[None, :]).astype(jnp.int32), axis=1)
    pos_gap = gap_start[seg] + k - (gap_end - gap_size)[seg]
    pos_all = jnp.concatenate([pos, pos_gap]).astype(jnp.int32)
    return pos_all.reshape(-1, 1, 2 * MOE_ROWS_STEP), tile_expert, tile_valid


ROW_DMA_UNROLL = 8


def _row_copy_burst(n_rows, row_copy):
    def start(j, c):
        row_copy(j, 0).start(priority=0)
        row_copy(j, 1).start(priority=1)
        return c

    def wait(j, c):
        row_copy(j, 0).wait()
        row_copy(j, 1).wait()
        return c

    lax.fori_loop(0, n_rows, start, 0, unroll=ROW_DMA_UNROLL)
    lax.fori_loop(0, n_rows, wait, 0, unroll=ROW_DMA_UNROLL)


def _scatter_rows_body(pos_ref, x_ref, xs_ref, zero_scr, sem, *, n_token_steps):
    def burst(src_ref):
        def row_copy(j, k):
            return pltpu.make_async_copy(src_ref.at[pl.ds(j, 1), :], xs_ref.at[pl.ds(pos_ref[0, 0, 2 * j + k], 1), :], sem)

        _row_copy_burst(x_ref.shape[0], row_copy)

    @pl.when(pl.program_id(0) < n_token_steps)
    def _():
        burst(x_ref)

    @pl.when(pl.program_id(0) >= n_token_steps)
    def _():
        zero_scr[...] = jnp.zeros_like(zero_scr)
        burst(zero_scr)


def scatter_rows(x, pos_all, n_rows_pad):
    t, d = x.shape
    tm = MOE_ROWS_STEP
    n_token_steps = t // tm
    assert pos_all.shape[0] * 2 * tm == n_rows_pad
    return pl.pallas_call(
        functools.partial(_scatter_rows_body, n_token_steps=n_token_steps), grid=(pos_all.shape[0],),
        in_specs=[pl.BlockSpec((1, 1, 2 * tm), lambda i: (i, 0, 0), memory_space=pltpu.SMEM),
                  pl.BlockSpec((tm, d), lambda i: (jnp.minimum(i, n_token_steps - 1), 0))],
        out_specs=pl.BlockSpec(memory_space=pl.ANY), out_shape=jax.ShapeDtypeStruct((n_rows_pad, d), F32),
        scratch_shapes=[pltpu.VMEM((tm, d), F32), pltpu.SemaphoreType.DMA(())],
        compiler_params=_cparams(1), name="moe_scatter_rows")(pos_all, x)


def _expert_body(te_ref, tv_ref, xs_ref, g_ref, wg_ref, wu_ref, wd_ref, o_ref, *, sub):
    del te_ref

    @pl.when(tv_ref[pl.program_id(0)] == 1)
    def _():
        o_ref[...] = _swiglu(_rms(xs_ref[...], g_ref[...]).astype(BF16), wg_ref, wu_ref, wd_ref, sub)

    @pl.when(tv_ref[pl.program_id(0)] == 0)
    def _():
        o_ref[...] = jnp.zeros_like(o_ref)


def grouped_swiglu(xs, gain, wg, wu, wd, tile_expert, tile_valid, sub=512):
    n_rows, d = xs.shape
    f_dim = wg.shape[2]
    grid_spec = pltpu.PrefetchScalarGridSpec(
        num_scalar_prefetch=2, grid=(n_rows // MOE_TILE,),
        in_specs=[pl.BlockSpec((MOE_TILE, d), lambda i, te, tv: (i, 0)),
                  pl.BlockSpec((1, d), lambda i, te, tv: (0, 0)),
                  pl.BlockSpec((1, d, f_dim), lambda i, te, tv: (te[i], 0, 0), pipeline_mode=pl.Buffered(1)),
                  pl.BlockSpec((1, d, f_dim), lambda i, te, tv: (te[i], 0, 0), pipeline_mode=pl.Buffered(1)),
                  pl.BlockSpec((1, f_dim, d), lambda i, te, tv: (te[i], 0, 0), pipeline_mode=pl.Buffered(1))],
        out_specs=pl.BlockSpec((MOE_TILE, d), lambda i, te, tv: (i, 0)))
    return pl.pallas_call(
        functools.partial(_expert_body, sub=sub), grid_spec=grid_spec,
        out_shape=jax.ShapeDtypeStruct((n_rows, d), F32),
        compiler_params=_cparams(1), name="moe_grouped_swiglu")(tile_expert, tile_valid, xs, gain.reshape(1, d), wg, wu, wd)


def _combine_body(*refs, final):
    it = iter(refs)
    pos_ref, pos_next_ref, x_ref, route_ref, p_ref, pg_ref, wpg_ref, wp_ref = (next(it) for _ in range(8))
    fg_ref = next(it) if final else None
    ys_ref, o_ref, buf, sems = (next(it) for _ in range(4))
    i = pl.program_id(0)
    n_rows = x_ref.shape[0]
    slot, next_slot = i % 2, (i + 1) % 2

    def row_copy(p_ref_, s, j, k):
        return pltpu.make_async_copy(ys_ref.at[pl.ds(p_ref_[0, 0, 2 * j + k], 1), :], buf.at[s, k, pl.ds(j, 1), :], sems.at[s])

    def wait_rows(p_ref_, s):
        def wait(j, c):
            row_copy(p_ref_, s, j, 0).wait()
            row_copy(p_ref_, s, j, 1).wait()
            return c

        lax.fori_loop(0, n_rows, wait, 0, unroll=ROW_DMA_UNROLL)

    @pl.when(i == 0)
    def _():
        def start(j, c):
            row_copy(pos_ref, 0, j, 0).start(priority=0)
            row_copy(pos_ref, 0, j, 1).start(priority=1)
            return c

        lax.fori_loop(0, n_rows, start, 0, unroll=ROW_DMA_UNROLL)

    wait_rows(pos_ref, slot)
    for j in range(n_rows):
        row_copy(pos_next_ref, next_slot, j, 0).start(priority=0)
        row_copy(pos_next_ref, next_slot, j, 1).start(priority=1)
    route = route_ref[...]
    moe = route[:, ROUTE_W1:ROUTE_W1 + 1] * buf[slot, 0] + route[:, ROUTE_W2:ROUTE_W2 + 1] * buf[slot, 1]
    o_ref[...] = _ple_epilogue(x_ref[...] + moe, p_ref, pg_ref, wpg_ref, wp_ref, fg_ref)

    @pl.when(i == pl.num_programs(0) - 1)
    def _():
        wait_rows(pos_next_ref, next_slot)


def combine_ple(x, route, pos, ys, p, ple_gain, w_ple_gate, w_ple, final_gain=None):
    t, d = x.shape
    tm = MOE_ROWS_STEP
    final = final_gain is not None
    n_steps = t // tm
    in_specs = [pl.BlockSpec((1, 1, 2 * tm), lambda i: (i, 0, 0), memory_space=pltpu.SMEM),
                pl.BlockSpec((1, 1, 2 * tm), lambda i: (jnp.minimum(i + 1, n_steps - 1), 0, 0), memory_space=pltpu.SMEM),
                pl.BlockSpec((tm, d), lambda i: (i, 0)), pl.BlockSpec((tm, LANES), lambda i: (i, 0)),
                pl.BlockSpec((tm, p.shape[1]), lambda i: (i, 0)), pl.BlockSpec((1, d), lambda i: (0, 0)),
                pl.BlockSpec((d, d), lambda i: (0, 0), pipeline_mode=pl.Buffered(1)),
                pl.BlockSpec((p.shape[1], d), lambda i: (0, 0), pipeline_mode=pl.Buffered(1))]
    args = [pos, pos, x, route, p, ple_gain.reshape(1, d), w_ple_gate, w_ple]
    if final:
        in_specs.append(pl.BlockSpec((1, d), lambda i: (0, 0)))
        args.append(final_gain.reshape(1, d))
    in_specs.append(pl.BlockSpec(memory_space=pl.ANY))
    args.append(ys)
    return pl.pallas_call(
        functools.partial(_combine_body, final=final), grid=(n_steps,), in_specs=in_specs,
        out_specs=pl.BlockSpec((tm, d), lambda i: (i, 0)), out_shape=jax.ShapeDtypeStruct((t, d), F32),
        scratch_shapes=[pltpu.VMEM((2, 2, tm, d), F32), pltpu.SemaphoreType.DMA((2,))],
        compiler_params=_cparams(1), name="moe_combine_ple")(*args)


def moe_ple(x, ffn_gain, w_router, wg, wu, wd, p, ple_gain, w_ple_gate, w_ple, final_gain=None):
    t = x.shape[0]
    n_rows_pad = 2 * t + N_EXPERTS * MOE_TILE
    route = moe_route(x, ffn_gain, w_router)
    pos, tile_expert, tile_valid = _row_positions(route, n_rows_pad)
    xs = scatter_rows(x, pos, n_rows_pad)
    ys = grouped_swiglu(xs, ffn_gain, wg, wu, wd, tile_expert, tile_valid)
    return combine_ple(x, route, pos, ys, p, ple_gain, w_ple_gate, w_ple, final_gain)


CONV_PAD = 8


def _causal_conv_silu(xf, w_ref, pad_ref, bias=None):
    k_w = w_ref.shape[0]
    s = xf.shape[0]
    pad_ref[0:CONV_PAD, :] = jnp.zeros((CONV_PAD, xf.shape[1]), F32)
    pad_ref[CONV_PAD:CONV_PAD + s, :] = xf
    y = xf * w_ref[k_w - 1:k_w, :]
    for j in range(k_w - 1):
        start = CONV_PAD - (k_w - 1 - j)
        y = y + pad_ref[start:start + s, :] * w_ref[j:j + 1, :]
    if bias is not None:
        y = y + bias
    return _silu(y)


def _chunk_cumsum(x, rows):
    s = 1
    while s < CHUNK:
        x = x + jnp.where(rows % CHUNK >= s, pltpu.roll(x, s, 0), 0.0)
        s *= 2
    return x


def _tri_iotas():
    return lax.broadcasted_iota(jnp.int32, (CHUNK, CHUNK), 0), lax.broadcasted_iota(jnp.int32, (CHUNK, CHUNK), 1)


def _decay_from(g_col, g_row, incl):
    return jnp.where(incl, jnp.exp(jnp.where(incl, g_col - g_row, 0.0)), 0.0)


def _decay_matrix(gm_b, incl):
    return _decay_from(gm_b, gm_b.T, incl)


def _unit_lower_inverses(mats, ri, ci):
    eye = (ri == ci).astype(F32)
    first = (ri - ci == 1) & (ri % 2 == 1)
    ts = [eye - jnp.where(first, a, 0.0) for a in mats]
    s = 2
    while s < CHUNK:
        off_mask = ((ri // s) % 2 == 1) & (ci // s == ri // s - 1)
        tbs = [t.astype(BF16) for t in ts]
        inner = [_dot(jnp.where(off_mask, a, 0.0).astype(BF16), tb).astype(BF16) for a, tb in zip(mats, tbs)]
        outer = [_dot(tb, x) for tb, x in zip(tbs, inner)]
        ts = [t - x for t, x in zip(ts, outer)]
        s *= 2
    return ts


def _dn_body(alog_ref, dtb_ref, q_ref, k_ref, v_ref, z_ref, sm_ref, cwq_ref, cwk_ref, cwv_ref, ng_ref, o_ref,
             qn_scr, kn_scr, vc_scr, p_scr, n_scr, qp_scr, o0_scr, gl_scr, pad_scr, gall_scr, ball_scr):
    s = q_ref.shape[0]
    hd = DN_HEAD_DIM
    n_chunks = s // CHUNK
    ri, ci = _tri_iotas()
    incl = ri >= ci
    strict = ri > ci

    def l2n(x):
        return x * lax.rsqrt(jnp.sum(x * x, axis=-1, keepdims=True) + RMS_EPS)

    @pl.when(pl.program_id(1) == 0)
    def _():
        rows = lax.broadcasted_iota(jnp.int32, (s, LANES), 0)
        sm = sm_ref[...]
        gall_scr[...] = _chunk_cumsum(-jnp.exp(alog_ref[...]) * _softplus(sm + dtb_ref[...]), rows)
        ball_scr[...] = jax.nn.sigmoid(sm)

    for hh in range(DN_HEADS_PER_STEP):
        h = pl.program_id(1) * DN_HEADS_PER_STEP + hh
        hs = slice(hh * hd, (hh + 1) * hd)
        qn_scr[...] = l2n(_causal_conv_silu(q_ref[:, hs].astype(F32), cwq_ref.at[:, hs], pad_scr)) * hd ** -0.5
        kn_scr[...] = l2n(_causal_conv_silu(k_ref[:, hs].astype(F32), cwk_ref.at[:, hs], pad_scr))
        vc_scr[...] = _causal_conv_silu(v_ref[:, hs].astype(F32), cwv_ref.at[:, hs], pad_scr)

        def factor_group(cg, carry, h=h):
            us = range(DN_UNROLL)
            sls = [pl.ds(pl.multiple_of((cg * DN_UNROLL + u) * CHUNK, CHUNK), CHUNK) for u in us]
            kc = [kn_scr[sl, :] for sl in sls]

            def head_lane(ref, sl, l):
                return jnp.broadcast_to(jnp.sum(jnp.where(ci == l, ref[sl, :], 0.0), axis=1, keepdims=True), (CHUNK, LANES))

            gm = [head_lane(gall_scr, sl, h) for sl in sls]
            bt = [head_lane(ball_scr, sl, DN_HEADS + h) for sl in sls]
            decay = [_decay_matrix(gm[u], incl) for u in us]
            kb = [kc[u] * bt[u] for u in us]
            kcb = [kc[u].astype(BF16) for u in us]
            kk = [_dot_nt(kb[u].astype(BF16), kcb[u]) for u in us]
            qk = [_dot_nt(qn_scr[sls[u], :].astype(BF16), kcb[u]) for u in us]
            t_inv = _unit_lower_inverses([jnp.where(strict, kk[u] * decay[u], 0.0) for u in us], ri, ci)
            eg = [jnp.exp(gm[u]) for u in us]
            rhs = [jnp.concatenate([kb[u] * eg[u], vc_scr[sls[u], :] * bt[u]], axis=1).astype(BF16) for u in us]
            wu = [_dot(t_inv[u].astype(BF16), rhs[u]).astype(BF16) for u in us]
            g_last = [gm[u][CHUNK - 1:CHUNK, :] for u in us]
            pn = [_dot_tn((kc[u] * jnp.exp(g_last[u] - gm[u])).astype(BF16), wu[u]) for u in us]
            qo = [_dot((qk[u] * decay[u]).astype(BF16), wu[u]) for u in us]
            for u in us:
                p_scr[h, sls[u], :] = pn[u][:, :hd].astype(BF16)
                n_scr[h, sls[u], :] = pn[u][:, hd:]
                qp_scr[h, sls[u], :] = (qn_scr[sls[u], :] * eg[u] - qo[u][:, :hd]).astype(BF16)
                o0_scr[h, sls[u], :] = qo[u][:, hd:].astype(BF16)
                gl_scr[h, pl.ds(pl.multiple_of((cg * DN_UNROLL + u) * 8, 8), 8), :] = jnp.broadcast_to(
                    jnp.exp(g_last[u]), (8, LANES))
            return carry

        lax.fori_loop(0, n_chunks // DN_UNROLL, factor_group, 0)

    @pl.when(pl.program_id(1) == pl.num_programs(1) - 1)
    def _():
        def scan(c, states):
            sl = pl.ds(pl.multiple_of(c * CHUNK, CHUNK), CHUNK)
            sbs = [st.astype(BF16) for st in states]
            outs = [_dot(qp_scr[h, sl, :], sbs[h]) for h in range(DN_HEADS)]
            decs = [_dot(p_scr[h, sl, :], sbs[h]) for h in range(DN_HEADS)]
            new = []
            for h in range(DN_HEADS):
                hs = slice(h * hd, (h + 1) * hd)
                e_last = gl_scr[h, pl.ds(pl.multiple_of(c * 8, 8), 1), :]
                new.append(states[h] * e_last - decs[h] + n_scr[h, sl, :])
                o = outs[h] + o0_scr[h, sl, :]
                on = o * lax.rsqrt(jnp.mean(o * o, axis=-1, keepdims=True) + RMS_EPS) * ng_ref[...]
                o_ref[sl, hs] = (on * _silu(z_ref[sl, hs].astype(F32))).astype(BF16)
            return tuple(new)

        lax.fori_loop(0, n_chunks, scan, tuple(jnp.zeros((hd, hd), F32) for _ in range(DN_HEADS)))


def gated_deltanet(pr_dn, small, conv_w, a_log, dt_bias, norm_g, bsz, s):
    t = bsz * s
    hd = DN_HEAD_DIM
    n_steps = DN_HEADS // DN_HEADS_PER_STEP
    assert (s // CHUNK) % DN_UNROLL == 0

    def col(j0):
        return pl.BlockSpec((s, DN_STEP_W), lambda b, h, j0=j0: (b, j0 + h))

    def cw(j0):
        return pl.BlockSpec((CONV_WIDTH, DN_STEP_W), lambda b, h, j0=j0: (0, j0 + h))

    def per_head(dtype):
        return pltpu.VMEM((DN_HEADS, s, hd), dtype)

    def lanes(v):
        return jnp.zeros((1, LANES), F32).at[0, :DN_HEADS].set(v)

    lane_vec = pl.BlockSpec((1, LANES), lambda b, h: (0, 0))
    return pl.pallas_call(
        _dn_body, grid=(bsz, n_steps),
        in_specs=[lane_vec, lane_vec, col(0), col(n_steps), col(2 * n_steps),
                  pl.BlockSpec((s, DN_WIDTH), lambda b, h: (b, 3), pipeline_mode=pl.Buffered(1)),
                  pl.BlockSpec((s, LANES), lambda b, h: (b, 0)),
                  cw(0), cw(n_steps), cw(2 * n_steps), pl.BlockSpec((1, hd), lambda b, h: (0, 0))],
        out_specs=pl.BlockSpec((s, DN_WIDTH), lambda b, h: (b, 0)),
        out_shape=jax.ShapeDtypeStruct((t, DN_WIDTH), BF16),
        scratch_shapes=[pltpu.VMEM((s, hd), F32)] * 3 + [per_head(BF16), per_head(F32), per_head(BF16), per_head(BF16),
                                                        pltpu.VMEM((DN_HEADS, s // CHUNK * 8, LANES), F32),
                                                        pltpu.VMEM((CONV_PAD + s, hd), F32),
                                                        pltpu.VMEM((s, LANES), F32), pltpu.VMEM((s, LANES), F32)],
        compiler_params=_cparams(2), name="gated_deltanet")(
            lanes(a_log), lanes(dt_bias), pr_dn, pr_dn, pr_dn, pr_dn, small, conv_w, conv_w, conv_w, norm_g.reshape(1, hd))


def _expand_heads(v, g, n_rows):
    lane_head = lax.broadcasted_iota(jnp.int32, (n_rows, SSM_GW), 1) // SSM_HEAD_DIM
    out = jnp.zeros((n_rows, SSM_GW), F32)
    for hh in range(SSM_HPG):
        l0 = DT_LANE0 + g * SSM_HPG + hh
        out = jnp.where(lane_head == hh, v[:, l0:l0 + 1], out)
    return out


def _ssd_body(pr_ref, sm_ref, cw_ref, cb_ref, alog_ref, dtb_ref, d_ref, ng_ref, o_ref,
              x_scr, b_scr, c_scr, dt_scr, acs_scr, st_scr, pad_scr):
    s = pr_ref.shape[0]
    rows = lax.broadcasted_iota(jnp.int32, (s, LANES), 0)
    n_state = SSM_GROUPS * SSM_STATE
    k_w = cw_ref.shape[0]
    nr = SSM_CONV_ROWS

    def conv_rows(i, carry):
        rsl = pl.ds(pl.multiple_of(i * nr, nr), nr)
        for j in range(SSM_XBC // LANES):
            cs = slice(j * LANES, (j + 1) * LANES)
            src = j * LANES if j * LANES < SSM_WIDTH else j * LANES + SSM_WIDTH
            xf = pr_ref[rsl, src:src + LANES].astype(F32)
            pad_scr[0:CONV_PAD, cs] = pad_scr[nr:nr + CONV_PAD, cs]
            pad_scr[CONV_PAD:CONV_PAD + nr, cs] = xf
            y = xf * cw_ref[k_w - 1:k_w, cs] + cb_ref[:, cs]
            for tap in range(k_w - 1):
                start = CONV_PAD - (k_w - 1 - tap)
                y = y + pad_scr[start:start + nr, cs] * cw_ref[tap:tap + 1, cs]
            y = _silu(y)
            if j * LANES < SSM_WIDTH:
                x_scr[rsl, cs] = y
            elif j * LANES < SSM_WIDTH + n_state:
                b_scr[rsl, j * LANES - SSM_WIDTH:(j + 1) * LANES - SSM_WIDTH] = y.astype(BF16)
            else:
                c_scr[rsl, j * LANES - SSM_WIDTH - n_state:(j + 1) * LANES - SSM_WIDTH - n_state] = y.astype(BF16)
        return carry

    pad_scr[nr:nr + CONV_PAD, :] = jnp.zeros((CONV_PAD, SSM_XBC), F32)
    lax.fori_loop(0, s // nr, conv_rows, 0)
    dt = _softplus(sm_ref[...] + dtb_ref[...])
    dt_scr[...] = dt
    acs_scr[...] = _chunk_cumsum(dt * (-jnp.exp(alog_ref[...])), rows)
    st_scr[...] = jnp.zeros_like(st_scr)

    ri, ci = _tri_iotas()
    incl = ri >= ci

    groups = range(SSM_GROUPS)

    def chunk_terms(c):
        sl = pl.ds(pl.multiple_of(c * CHUNK, CHUNK), CHUNK)
        dt_c, acs_c = dt_scr[sl, :], acs_scr[sl, :]
        acs_last = acs_c[CHUNK - 1:CHUNK, :]
        acs_t = acs_c.T
        bc = [b_scr[sl, g * SSM_STATE:(g + 1) * SSM_STATE] for g in groups]
        cc = [c_scr[sl, g * SSM_STATE:(g + 1) * SSM_STATE] for g in groups]
        xg = [x_scr[sl, g * SSM_GW:(g + 1) * SSM_GW] for g in groups]
        xdt = [xg[g] * _expand_heads(dt_c, g, CHUNK) for g in groups]
        acs_e = [_expand_heads(acs_c, g, CHUNK) for g in groups]
        last_e = [_expand_heads(acs_last, g, 1) for g in groups]
        cbm = [_dot_nt(cc[g], bc[g]) for g in groups]
        y_diag = []
        for g in groups:
            xdt_b = xdt[g].astype(BF16)
            for hh in range(SSM_HPG):
                l0 = DT_LANE0 + g * SSM_HPG + hh
                lm = _decay_from(jnp.broadcast_to(acs_c[:, l0:l0 + 1], (CHUNK, CHUNK)),
                                 jnp.broadcast_to(acs_t[l0:l0 + 1, :], (CHUNK, CHUNK)), incl)
                y_diag.append(_dot((cbm[g] * lm).astype(BF16), xdt_b[:, hh * SSM_HEAD_DIM:(hh + 1) * SSM_HEAD_DIM]))
        st_new = [_dot_tn(bc[g], (xdt[g] * jnp.exp(last_e[g] - acs_e[g])).astype(BF16)) for g in groups]
        y_local = [jnp.concatenate(y_diag[g * SSM_HPG:(g + 1) * SSM_HPG], axis=1) + _expand_heads(d_ref[...], g, 1) * xg[g]
                   for g in groups]
        return sl, cc, acs_e, last_e, st_new, y_local

    def chunks(it, carry):
        terms = [chunk_terms(it * SSM_LOCKSTEP + u) for u in range(SSM_LOCKSTEP)]
        state = [st_scr[g] for g in groups]
        for sl, cc, acs_e, last_e, st_new, y_local in terms:
            y_off = [_dot(cc[g], state[g].astype(BF16)) for g in groups]
            state = [state[g] * jnp.exp(last_e[g]) + st_new[g] for g in groups]
            for g in groups:
                gs = slice(g * SSM_GW, (g + 1) * SSM_GW)
                y = y_off[g] * jnp.exp(acs_e[g]) + y_local[g]
                y = y * _silu(pr_ref[sl, SSM_WIDTH + g * SSM_GW:SSM_WIDTH + (g + 1) * SSM_GW].astype(F32))
                y = y * lax.rsqrt(jnp.mean(y * y, axis=-1, keepdims=True) + RMS_EPS) * ng_ref[:, gs]
                o_ref[sl, gs] = y.astype(BF16)
        for g in groups:
            st_scr[g] = state[g]
        return carry

    lax.fori_loop(0, s // CHUNK // SSM_LOCKSTEP, chunks, 0)


def mamba2_ssd(pr_ssm, small, conv_w, conv_b, a_log, dt_bias, d_skip, norm_g, bsz, s):
    t = bsz * s

    def lanes(v):
        return jnp.zeros((1, LANES), F32).at[0, DT_LANE0:DT_LANE0 + SSM_HEADS].set(v)

    def whole(shape):
        return pl.BlockSpec(shape, lambda b: (0, 0))

    n_state = SSM_GROUPS * SSM_STATE
    return pl.pallas_call(
        _ssd_body, grid=(bsz,),
        in_specs=[pl.BlockSpec((s, pr_ssm.shape[1]), lambda b: (b, 0)),
                  pl.BlockSpec((s, LANES), lambda b: (b, 0)),
                  whole((CONV_WIDTH, SSM_XBC)), whole((1, SSM_XBC)), whole((1, LANES)), whole((1, LANES)), whole((1, LANES)),
                  whole((1, SSM_WIDTH))],
        out_specs=pl.BlockSpec((s, SSM_WIDTH), lambda b: (b, 0)),
        out_shape=jax.ShapeDtypeStruct((t, SSM_WIDTH), BF16),
        scratch_shapes=[pltpu.VMEM((s, SSM_WIDTH), F32), pltpu.VMEM((s, n_state), BF16), pltpu.VMEM((s, n_state), BF16),
                        pltpu.VMEM((s, LANES), F32), pltpu.VMEM((s, LANES), F32),
                        pltpu.VMEM((SSM_GROUPS, SSM_STATE, SSM_GW), F32),
                        pltpu.VMEM((CONV_PAD + SSM_CONV_ROWS, SSM_XBC), F32)],
        compiler_params=_cparams(1), name="mamba2_ssd")(
            pr_ssm, small, conv_w, conv_b.reshape(1, -1), lanes(a_log), lanes(dt_bias), lanes(d_skip), norm_g.reshape(1, -1))


def _attn_body(pr_ref, o_ref, qf, kf, vf, og, lg):
    s = pr_ref.shape[0]
    c = ATTN_BLOCK
    e = ATTN_HEAD_DIM
    ii = lax.broadcasted_iota(jnp.int32, (c, c), 0)
    jj = lax.broadcasted_iota(jnp.int32, (c, c), 1)
    lane_head = lax.broadcasted_iota(jnp.int32, (c, LANES), 1) // e
    for gi, (window, dil) in enumerate(DILATION_GROUPS):
        w_sub = window // dil
        nb = s // dil // c
        for j in range(HALVES):
            c0 = gi * ATTN_OUT_WIDTH + j * LANES
            qf[j] = pr_ref[:, c0:c0 + LANES].astype(F32) * e ** -0.5
            kf[j] = pr_ref[:, ATTN_QKV + c0:ATTN_QKV + c0 + LANES].astype(F32)
            vf[j] = pr_ref[:, 2 * ATTN_QKV + c0:2 * ATTN_QKV + c0 + LANES].astype(F32)
        d_cur = ii - jj
        d_prev = d_cur + c
        ok_cur = (d_cur >= 0) & (d_cur <= w_sub)
        ok_prev_static = d_prev <= w_sub

        def blocks(it, carry, gi=gi, dil=dil, nb=nb, d_cur=d_cur, d_prev=d_prev, ok_cur=ok_cur, ok_prev_static=ok_prev_static):
            cur, prev, ok_prev = [], [], []
            for b in range(ATTN_LOCKSTEP):
                idx = it * ATTN_LOCKSTEP + b
                r, n = idx // nb, idx % nb
                cur.append(pl.ds(r + n * (c * dil), c, stride=dil))
                prev.append(pl.ds(r + jnp.maximum(n - 1, 0) * (c * dil), c, stride=dil))
                ok_prev.append(ok_prev_static & (n > 0))
            units = [(b, j, hh) for b in range(ATTN_LOCKSTEP) for j in range(HALVES) for hh in range(HEADS_PER_HALF)]
            def values(rows_):
                return [[[jnp.where(lane_head == hh, vf.at[j][rows_[b], :], 1.0).astype(BF16) for hh in range(HEADS_PER_HALF)]
                         for j in range(HALVES)] for b in range(ATTN_LOCKSTEP)]

            vcur, vprev = values(cur), values(prev)
            s_cur, s_prev = [], []
            for b in range(ATTN_LOCKSTEP):
                for j in range(HALVES):
                    q = qf.at[j][cur[b], :]
                    kc, kp = kf.at[j][cur[b], :].astype(BF16), kf.at[j][prev[b], :].astype(BF16)
                    for hh in range(HEADS_PER_HALF):
                        qh = jnp.where(lane_head == hh, q, 0.0).astype(BF16)
                        s_cur.append(_dot_nt(qh, kc))
                        s_prev.append(_dot_nt(qh, kp))
            probs, maxes = [], []
            for i, (b, j, hh) in enumerate(units):
                head = gi * ATTN_GROUP_HEADS + j * HEADS_PER_HALF + hh
                slope = float(2.0 ** (-ALIBI_MAX_BIAS * (head + 1.0) / ATTN_HEADS)) * dil
                sc = jnp.where(ok_cur, s_cur[i] - slope * d_cur.astype(F32), NEG_BIG)
                sp = jnp.where(ok_prev[b], s_prev[i] - slope * d_prev.astype(F32), NEG_BIG)
                m = jnp.max(jnp.maximum(sc, sp), axis=1, keepdims=True)
                probs.append((jnp.exp(sc - m).astype(BF16), jnp.exp(sp - m).astype(BF16)))
                maxes.append(m)
            pv = [_dot(probs[i][0], vcur[b][j][hh]) + _dot(probs[i][1], vprev[b][j][hh]) for i, (b, j, hh) in enumerate(units)]
            for b in range(ATTN_LOCKSTEP):
                for j in range(HALVES):
                    i0 = (b * HALVES + j) * HEADS_PER_HALF
                    first = lane_head == 0
                    num = jnp.where(first, pv[i0], pv[i0 + 1])
                    den = pltpu.roll(jnp.where(first, pv[i0 + 1], pv[i0]), e, 1)
                    og.at[gi, j][cur[b], :] = num / den
                    lg.at[gi, j][cur[b], :] = jnp.where(first, maxes[i0], maxes[i0 + 1]) + jnp.log(den)
            return carry

        lax.fori_loop(0, dil * nb // ATTN_LOCKSTEP, blocks, 0)
    for j in range(HALVES):
        m = jnp.maximum(jnp.maximum(lg[0, j], lg[1, j]), lg[2, j])
        num = jnp.zeros((s, LANES), F32)
        den = jnp.zeros((s, LANES), F32)
        for gi in range(N_DIL):
            w = jnp.exp(lg[gi, j] - m)
            num = num + w * og[gi, j]
            den = den + w
        o_ref[:, j * LANES:(j + 1) * LANES] = (num / den).astype(BF16)


def dilated_attention(pr_attn, bsz, s):
    t = bsz * s
    assert N_DIL == 3 and HEADS_PER_HALF == 2
    for window, dil in DILATION_GROUPS:
        assert s % (dil * ATTN_BLOCK) == 0 and window // dil <= ATTN_BLOCK
    return pl.pallas_call(
        _attn_body, grid=(bsz,),
        in_specs=[pl.BlockSpec((s, pr_attn.shape[1]), lambda b: (b, 0))],
        out_specs=pl.BlockSpec((s, ATTN_OUT_WIDTH), lambda b: (b, 0)),
        out_shape=jax.ShapeDtypeStruct((t, ATTN_OUT_WIDTH), BF16),
        scratch_shapes=[pltpu.VMEM((HALVES, s, LANES), F32)] * 3 + [pltpu.VMEM((N_DIL, HALVES, s, LANES), F32)] * 2,
        compiler_params=_cparams(1), name="dilated_attention")(pr_attn)


def _split_w_in(w_in):
    cuts = np.cumsum(IN_SPLIT_SIZES)[:-1].tolist()
    dn_qkv, dn_z, dn_a, dn_b, ssm_xbc, ssm_z, ssm_dt, attn_qkv, gate = jnp.split(w_in, cuts, axis=-1)
    w_dn = jnp.concatenate([dn_qkv, dn_z], axis=1).astype(BF16)
    w_ssm = jnp.concatenate([ssm_xbc[:, :SSM_WIDTH], ssm_z, ssm_xbc[:, SSM_WIDTH:]], axis=1).astype(BF16)
    small = jnp.concatenate([dn_a, dn_b, ssm_dt], axis=1)
    w_small = jnp.pad(small, ((0, 0), (0, LANES - small.shape[1]))).astype(BF16)
    return w_dn, w_ssm, attn_qkv.astype(BF16), gate.astype(BF16), w_small


def kernel(x, p, mix_norm, w_in, dn_conv, dn_a_log, dn_dt_bias, dn_norm, ssm_conv, ssm_conv_b, ssm_a_log, ssm_dt_bias, ssm_d, ssm_norm, w_br_dn, w_br_ssm, w_br_attn, w_out, ffn_norm, w_ff_gate, w_ff_up, w_ff_down, w_router, w_moe_gate, w_moe_up, w_moe_down, ple_norm, w_ple, w_ple_gate, final_norm):
    bsz, s, d = x.shape
    t = bsz * s
    depth = w_in.shape[0]
    x = x.reshape(t, d)
    for i in range(depth):
        w_dn, w_ssm, w_attn, w_gate, w_small = _split_w_in(w_in[i])
        g = mix_norm[i]
        pr_dn = norm_matmul(x, g, w_dn, 512)
        pr_ssm, small = norm_matmul(x, g, w_ssm, 512, w_small=w_small)
        pr_attn = norm_matmul(x, g, w_attn, 768)
        gates = norm_matmul(x, g, w_gate, 512)

        y_dn = gated_deltanet(pr_dn, small, dn_conv[i], dn_a_log[i], dn_dt_bias[i], dn_norm[i], bsz, s)
        y_ssm = mamba2_ssd(pr_ssm, small, ssm_conv[i], ssm_conv_b[i], ssm_a_log[i], ssm_dt_bias[i], ssm_d[i], ssm_norm[i], bsz, s)
        y_attn = dilated_attention(pr_attn, bsz, s)

        x = merge_out(x, y_dn, y_ssm, y_attn, gates,
                      w_br_dn[i].astype(BF16), w_br_ssm[i].astype(BF16), w_br_attn[i].astype(BF16), w_out[i].astype(BF16))

        j = i // 2
        fin = final_norm if i == depth - 1 else None
        common = (p[i].reshape(t, -1), ple_norm[i], w_ple_gate[i].astype(BF16), w_ple[i].astype(BF16))
        if i % 2 == 0:
            x = ffn_ple(x, ffn_norm[i], w_ff_gate[j][None].astype(BF16), w_ff_up[j][None].astype(BF16),
                        w_ff_down[j][None].astype(BF16), *common, tm=1024, sub=256, final_gain=fin)
        else:
            wr = jnp.pad(w_router[j], ((0, 0), (0, LANES - N_EXPERTS)))
            x = moe_ple(x, ffn_norm[i], wr, w_moe_gate[j].astype(BF16), w_moe_up[j].astype(BF16), w_moe_down[j].astype(BF16),
                        *common, final_gain=fin)
    return x.reshape(bsz, s, d)
```

```python
import functools

import numpy as np
import jax
import jax.numpy as jnp
from jax import lax
from jax.experimental import pallas as pl
from jax.experimental.pallas import tpu as pltpu

F32 = jnp.float32
BF16 = jnp.bfloat16

D_MODEL = 1024
PLE_DIM = 256
CONV_WIDTH = 4
RMS_EPS = 1e-6
DN_HEADS = 6
DN_HEAD_DIM = 128
DN_WIDTH = DN_HEADS * DN_HEAD_DIM
SSM_HEADS = 12
SSM_HEAD_DIM = 64
SSM_WIDTH = SSM_HEADS * SSM_HEAD_DIM
SSM_GROUPS = 2
SSM_STATE = 128
SSM_XBC = SSM_WIDTH + 2 * SSM_GROUPS * SSM_STATE
SSM_GW = SSM_WIDTH // SSM_GROUPS
SSM_HPG = SSM_HEADS // SSM_GROUPS
ATTN_HEADS = 12
ATTN_HEAD_DIM = 64
ATTN_QKV = ATTN_HEADS * ATTN_HEAD_DIM
DILATION_GROUPS = ((128, 1), (512, 4), (2048, 16))
N_DIL = len(DILATION_GROUPS)
ATTN_GROUP_HEADS = ATTN_HEADS // N_DIL
ATTN_OUT_WIDTH = ATTN_GROUP_HEADS * ATTN_HEAD_DIM
ATTN_BLOCK = 128
ALIBI_MAX_BIAS = 8.0
N_BRANCHES = 3
FFN_DIM = 2816
N_EXPERTS = 8
EXPERT_DIM = 3584
IN_SPLIT_SIZES = (3 * DN_WIDTH, DN_WIDTH, DN_HEADS, DN_HEADS, SSM_XBC, SSM_WIDTH, SSM_HEADS, 3 * ATTN_QKV, N_BRANCHES * D_MODEL)

LANES = 128
HALVES = ATTN_OUT_WIDTH // LANES
HEADS_PER_HALF = LANES // ATTN_HEAD_DIM
DT_LANE0 = 2 * DN_HEADS
CHUNK = 128
DN_HEADS_PER_STEP = 2
DN_UNROLL = 16
DN_STEP_W = DN_HEADS_PER_STEP * DN_HEAD_DIM
SSM_CONV_ROWS = 256
SSM_LOCKSTEP = 2
ATTN_LOCKSTEP = 4
NEG_BIG = -1e30
VMEM_LIMIT_BYTES = 56 * 1024 * 1024


def _cparams(n_axes):
    return pltpu.CompilerParams(dimension_semantics=("arbitrary",) * n_axes, vmem_limit_bytes=VMEM_LIMIT_BYTES)


def _rms(x, g):
    return x * lax.rsqrt(jnp.mean(x * x, axis=-1, keepdims=True) + RMS_EPS) * g


def _dot(a, b):
    return jnp.dot(a, b, preferred_element_type=F32)


def _dot_nt(a, b):
    return lax.dot_general(a, b, (((1,), (1,)), ((), ())), preferred_element_type=F32)


def _dot_tn(a, b):
    return lax.dot_general(a, b, (((0,), (0,)), ((), ())), preferred_element_type=F32)


def _silu(x):
    return x * jax.nn.sigmoid(x)


def _softplus(x):
    return jnp.maximum(x, 0.0) + jnp.log1p(jnp.exp(-jnp.abs(x)))


def _norm_matmul_body(x_ref, g_ref, w_ref, *rest, chunk, with_small):
    if with_small:
        ws_ref, o_ref, os_ref = rest
    else:
        (o_ref,) = rest
    hn = _rms(x_ref[...], g_ref[...]).astype(BF16)
    for c in range(w_ref.shape[1] // chunk):
        sl = slice(c * chunk, (c + 1) * chunk)
        o_ref[:, sl] = _dot(hn, w_ref[:, sl]).astype(BF16)
    if with_small:
        os_ref[...] = _dot(hn, ws_ref[...])


def norm_matmul(x, gain, w, chunk, w_small=None, tm=1024):
    t, d = x.shape
    n = w.shape[1]
    with_small = w_small is not None
    in_specs = [pl.BlockSpec((tm, d), lambda i: (i, 0)),
                pl.BlockSpec((1, d), lambda i: (0, 0)),
                pl.BlockSpec((d, n), lambda i: (0, 0), pipeline_mode=pl.Buffered(1))]
    out_specs = [pl.BlockSpec((tm, n), lambda i: (i, 0))]
    out_shape = [jax.ShapeDtypeStruct((t, n), BF16)]
    args = [x, gain.reshape(1, d), w]
    if with_small:
        in_specs.append(pl.BlockSpec((d, LANES), lambda i: (0, 0)))
        out_specs.append(pl.BlockSpec((tm, LANES), lambda i: (i, 0)))
        out_shape.append(jax.ShapeDtypeStruct((t, LANES), F32))
        args.append(w_small)
    outs = pl.pallas_call(
        functools.partial(_norm_matmul_body, chunk=chunk, with_small=with_small),
        grid=(t // tm,), in_specs=in_specs, out_specs=out_specs, out_shape=out_shape,
        compiler_params=_cparams(1), name="norm_matmul")(*args)
    return outs if with_small else outs[0]


def _merge_body(x_ref, g_ref, wgate_ref, ydn_ref, yssm_ref, yattn_ref, wdn_ref, wssm_ref, wattn_ref, wout_ref, o_ref):
    d = x_ref.shape[1]
    x = x_ref[...]
    hn = _rms(x, g_ref[...]).astype(BF16)

    def sig(j):
        return jax.nn.sigmoid(_dot(hn, wgate_ref[:, j * d:(j + 1) * d]))

    m = sig(0) * _dot(ydn_ref[...], wdn_ref[...])
    m = m + sig(1) * _dot(yssm_ref[...], wssm_ref[...])
    m = m + sig(2) * _dot(yattn_ref[...], wattn_ref[...])
    o_ref[...] = x + _dot(m.astype(BF16), wout_ref[...])


def merge_out(x, gain, w_gate, y_dn, y_ssm, y_attn, w_dn, w_ssm, w_attn, w_out, tm=1024):
    t, d = x.shape

    def rows(width):
        return pl.BlockSpec((tm, width), lambda i: (i, 0))

    def whole(a):
        return pl.BlockSpec(a.shape, lambda i: (0, 0), pipeline_mode=pl.Buffered(1))

    return pl.pallas_call(
        _merge_body, grid=(t // tm,),
        in_specs=[rows(d), pl.BlockSpec((1, d), lambda i: (0, 0)), whole(w_gate),
                  rows(y_dn.shape[1]), rows(y_ssm.shape[1]), rows(y_attn.shape[1]),
                  whole(w_dn), whole(w_ssm), whole(w_attn), whole(w_out)],
        out_specs=rows(d), out_shape=jax.ShapeDtypeStruct((t, d), F32),
        compiler_params=_cparams(1), name="merge_out")(x, gain.reshape(1, d), w_gate, y_dn, y_ssm, y_attn,
                                                       w_dn, w_ssm, w_attn, w_out)


def _swiglu(hn, wg_ref, wu_ref, wd_ref, sub):
    y = None
    for c in range(wg_ref.shape[2] // sub):
        sl = slice(c * sub, (c + 1) * sub)
        a = _dot(hn, wg_ref[0, :, sl])
        u = _dot(hn, wu_ref[0, :, sl])
        yc = _dot((_silu(a) * u).astype(BF16), wd_ref[0, sl, :])
        y = yc if y is None else y + yc
    return y


def _ple_epilogue(x1, p_ref, pg_ref, wpg_ref, wp_ref, fg_ref):
    gate = jax.nn.sigmoid(_dot(_rms(x1, pg_ref[...]).astype(BF16), wpg_ref[...]))
    x2 = x1 + _dot(p_ref[...].astype(BF16), wp_ref[...]) * gate
    return x2 if fg_ref is None else _rms(x2, fg_ref[...])


def _ffn_body(*refs, sub, final):
    it = iter(refs)
    x_ref, g_ref, wg_ref, wu_ref, wd_ref, p_ref, pg_ref, wpg_ref, wp_ref = (next(it) for _ in range(9))
    fg_ref = next(it) if final else None
    o_ref = next(it)
    x = x_ref[...]
    y = _swiglu(_rms(x, g_ref[...]).astype(BF16), wg_ref, wu_ref, wd_ref, sub)
    o_ref[...] = _ple_epilogue(x + y, p_ref, pg_ref, wpg_ref, wp_ref, fg_ref)


def ffn_ple(x, ffn_gain, wg, wu, wd, p, ple_gain, w_ple_gate, w_ple, *, tm, sub, final_gain=None):
    t, d = x.shape
    final = final_gain is not None

    def whole(a):
        return pl.BlockSpec(a.shape, lambda i: (0,) * a.ndim, pipeline_mode=pl.Buffered(1))

    in_specs = [pl.BlockSpec((tm, d), lambda i: (i, 0)), pl.BlockSpec((1, d), lambda i: (0, 0)),
                whole(wg), whole(wu), whole(wd),
                pl.BlockSpec((tm, p.shape[1]), lambda i: (i, 0)), pl.BlockSpec((1, d), lambda i: (0, 0)),
                whole(w_ple_gate), whole(w_ple)]
    args = [x, ffn_gain.reshape(1, d), wg, wu, wd, p, ple_gain.reshape(1, d), w_ple_gate, w_ple]
    if final:
        in_specs.append(pl.BlockSpec((1, d), lambda i: (0, 0)))
        args.append(final_gain.reshape(1, d))
    return pl.pallas_call(
        functools.partial(_ffn_body, sub=sub, final=final), grid=(t // tm,), in_specs=in_specs,
        out_specs=pl.BlockSpec((tm, d), lambda i: (i, 0)), out_shape=jax.ShapeDtypeStruct((t, d), F32),
        compiler_params=_cparams(1), name="ffn_ple")(*args)


ROUTE_I1, ROUTE_I2, ROUTE_W1, ROUTE_W2 = 8, 9, 10, 11
MOE_TILE = 512
MOE_ROWS_STEP = 512


def _route_body(x_ref, g_ref, wr_ref, o_ref):
    hn = _rms(x_ref[...], g_ref[...])
    w = wr_ref[...]
    hn_hi, w_hi = hn.astype(BF16), w.astype(BF16)
    hn_lo, w_lo = (hn - hn_hi.astype(F32)).astype(BF16), (w - w_hi.astype(F32)).astype(BF16)
    logits = _dot(hn_hi, w_hi) + (_dot(hn_hi, w_lo) + _dot(hn_lo, w_hi))
    lane = lax.broadcasted_iota(jnp.int32, logits.shape, 1)
    lg = jnp.where(lane < N_EXPERTS, logits, NEG_BIG)
    m1 = jnp.max(lg, axis=1, keepdims=True)
    i1 = jnp.min(jnp.where(lg == m1, lane, LANES), axis=1, keepdims=True)
    lg2 = jnp.where(lane == i1, NEG_BIG, lg)
    m2 = jnp.max(lg2, axis=1, keepdims=True)
    i2 = jnp.min(jnp.where(lg2 == m2, lane, LANES), axis=1, keepdims=True)
    e2 = jnp.exp(m2 - m1)
    w1 = 1.0 / (1.0 + e2)
    rec = jnp.where(lane == ROUTE_I1, i1.astype(F32), 0.0) + jnp.where(lane == ROUTE_I2, i2.astype(F32), 0.0)
    o_ref[...] = rec + jnp.where(lane == ROUTE_W1, w1, 0.0) + jnp.where(lane == ROUTE_W2, e2 * w1, 0.0)


def moe_route(x, gain, w_router, tm=1024):
    t, d = x.shape
    return pl.pallas_call(
        _route_body, grid=(t // tm,),
        in_specs=[pl.BlockSpec((tm, d), lambda i: (i, 0)), pl.BlockSpec((1, d), lambda i: (0, 0)),
                  pl.BlockSpec((d, LANES), lambda i: (0, 0))],
        out_specs=pl.BlockSpec((tm, LANES), lambda i: (i, 0)), out_shape=jax.ShapeDtypeStruct((t, LANES), F32),
        compiler_params=_cparams(1), name="moe_route")(x, gain.reshape(1, d), w_router)


def _row_positions(route, n_rows_pad):
    t = route.shape[0]
    ids = route[:, ROUTE_I1:ROUTE_I2 + 1].astype(jnp.int32).reshape(-1)
    onehot = (ids[:, None] == jnp.arange(N_EXPERTS, dtype=jnp.int32)[None, :]).astype(jnp.int32)
    csum = jnp.cumsum(onehot, axis=0)
    rank = jnp.sum((csum - onehot) * onehot, axis=1)
    counts = csum[-1]
    sizes = (counts + MOE_TILE - 1) // MOE_TILE * MOE_TILE
    ends = jnp.cumsum(sizes)
    pos = (ends - sizes)[ids] + rank
    tile_start = jnp.arange(n_rows_pad // MOE_TILE, dtype=jnp.int32) * MOE_TILE
    tile_valid = (tile_start < ends[-1]).astype(jnp.int32)
    tile_expert = jnp.minimum(jnp.sum((tile_start[:, None] >= ends[None, :]).astype(jnp.int32), axis=1), N_EXPERTS - 1)
    last_expert = jnp.max(jnp.where(sizes > 0, jnp.arange(N_EXPERTS, dtype=jnp.int32), 0))
    tile_expert = jnp.where(tile_valid == 1, tile_expert, last_expert)
    gap_start = jnp.concatenate([ends - sizes + counts, ends[-1:]])
    gap_size = jnp.concatenate([sizes - counts, n_rows_pad - ends[-1:]])
    gap_end = jnp.cumsum(gap_size)
    k = jnp.arange(n_rows_pad - 2 * t, dtype=jnp.int32)
    seg = jnp.sum((k[:, None] >= gap_end[None, :]).astype(jnp.int32), axis=1)
    pos_gap = gap_start[seg] + k - (gap_end - gap_size)[seg]
    pos_all = jnp.concatenate([pos, pos_gap]).astype(jnp.int32)
    return pos_all.reshape(-1, 1, 2 * MOE_ROWS_STEP), tile_expert, tile_valid


ROW_DMA_UNROLL = 8


def _row_copy_burst(n_rows, row_copy):
    def start(j, c):
        row_copy(j, 0).start(priority=0)
        row_copy(j, 1).start(priority=1)
        return c

    def wait(j, c):
        row_copy(j, 0).wait()
        row_copy(j, 1).wait()
        return c

    lax.fori_loop(0, n_rows, start, 0, unroll=ROW_DMA_UNROLL)
    lax.fori_loop(0, n_rows, wait, 0, unroll=ROW_DMA_UNROLL)


def _scatter_rows_body(pos_ref, x_ref, xs_ref, zero_scr, sem, *, n_token_steps):
    def burst(src_ref):
        def row_copy(j, k):
            return pltpu.make_async_copy(src_ref.at[pl.ds(j, 1), :], xs_ref.at[pl.ds(pos_ref[0, 0, 2 * j + k], 1), :], sem)

        _row_copy_burst(x_ref.shape[0], row_copy)

    @pl.when(pl.program_id(0) < n_token_steps)
    def _():
        burst(x_ref)

    @pl.when(pl.program_id(0) >= n_token_steps)
    def _():
        zero_scr[...] = jnp.zeros_like(zero_scr)
        burst(zero_scr)


def scatter_rows(x, pos_all, n_rows_pad):
    t, d = x.shape
    tm = MOE_ROWS_STEP
    n_token_steps = t // tm
    assert pos_all.shape[0] * 2 * tm == n_rows_pad
    return pl.pallas_call(
        functools.partial(_scatter_rows_body, n_token_steps=n_token_steps), grid=(pos_all.shape[0],),
        in_specs=[pl.BlockSpec((1, 1, 2 * tm), lambda i: (i, 0, 0), memory_space=pltpu.SMEM),
                  pl.BlockSpec((tm, d), lambda i: (jnp.minimum(i, n_token_steps - 1), 0))],
        out_specs=pl.BlockSpec(memory_space=pl.ANY), out_shape=jax.ShapeDtypeStruct((n_rows_pad, d), F32),
        scratch_shapes=[pltpu.VMEM((tm, d), F32), pltpu.SemaphoreType.DMA(())],
        compiler_params=_cparams(1), name="moe_scatter_rows")(pos_all, x)


def _expert_body(te_ref, tv_ref, xs_ref, g_ref, wg_ref, wu_ref, wd_ref, o_ref, *, sub):
    del te_ref

    @pl.when(tv_ref[pl.program_id(0)] == 1)
    def _():
        o_ref[...] = _swiglu(_rms(xs_ref[...], g_ref[...]).astype(BF16), wg_ref, wu_ref, wd_ref, sub)

    @pl.when(tv_ref[pl.program_id(0)] == 0)
    def _():
        o_ref[...] = jnp.zeros_like(o_ref)


def grouped_swiglu(xs, gain, wg, wu, wd, tile_expert, tile_valid, sub=512):
    n_rows, d = xs.shape
    f_dim = wg.shape[2]
    grid_spec = pltpu.PrefetchScalarGridSpec(
        num_scalar_prefetch=2, grid=(n_rows // MOE_TILE,),
        in_specs=[pl.BlockSpec((MOE_TILE, d), lambda i, te, tv: (i, 0)),
                  pl.BlockSpec((1, d), lambda i, te, tv: (0, 0)),
                  pl.BlockSpec((1, d, f_dim), lambda i, te, tv: (te[i], 0, 0), pipeline_mode=pl.Buffered(1)),
                  pl.BlockSpec((1, d, f_dim), lambda i, te, tv: (te[i], 0, 0), pipeline_mode=pl.Buffered(1)),
                  pl.BlockSpec((1, f_dim, d), lambda i, te, tv: (te[i], 0, 0), pipeline_mode=pl.Buffered(1))],
        out_specs=pl.BlockSpec((MOE_TILE, d), lambda i, te, tv: (i, 0)))
    return pl.pallas_call(
        functools.partial(_expert_body, sub=sub), grid_spec=grid_spec,
        out_shape=jax.ShapeDtypeStruct((n_rows, d), F32),
        compiler_params=_cparams(1), name="moe_grouped_swiglu")(tile_expert, tile_valid, xs, gain.reshape(1, d), wg, wu, wd)


def _combine_body(*refs, final):
    it = iter(refs)
    pos_ref, pos_next_ref, x_ref, route_ref, p_ref, pg_ref, wpg_ref, wp_ref = (next(it) for _ in range(8))
    fg_ref = next(it) if final else None
    ys_ref, o_ref, buf, sems = (next(it) for _ in range(4))
    i = pl.program_id(0)
    n_rows = x_ref.shape[0]
    slot, next_slot = i % 2, (i + 1) % 2

    def row_copy(p_ref_, s, j, k):
        return pltpu.make_async_copy(ys_ref.at[pl.ds(p_ref_[0, 0, 2 * j + k], 1), :], buf.at[s, k, pl.ds(j, 1), :], sems.at[s])

    def wait_rows(p_ref_, s):
        def wait(j, c):
            row_copy(p_ref_, s, j, 0).wait()
            row_copy(p_ref_, s, j, 1).wait()
            return c

        lax.fori_loop(0, n_rows, wait, 0, unroll=ROW_DMA_UNROLL)

    @pl.when(i == 0)
    def _():
        def start(j, c):
            row_copy(pos_ref, 0, j, 0).start(priority=0)
            row_copy(pos_ref, 0, j, 1).start(priority=1)
            return c

        lax.fori_loop(0, n_rows, start, 0, unroll=ROW_DMA_UNROLL)

    wait_rows(pos_ref, slot)
    for j in range(n_rows):
        row_copy(pos_next_ref, next_slot, j, 0).start(priority=0)
        row_copy(pos_next_ref, next_slot, j, 1).start(priority=1)
    route = route_ref[...]
    moe = route[:, ROUTE_W1:ROUTE_W1 + 1] * buf[slot, 0] + route[:, ROUTE_W2:ROUTE_W2 + 1] * buf[slot, 1]
    o_ref[...] = _ple_epilogue(x_ref[...] + moe, p_ref, pg_ref, wpg_ref, wp_ref, fg_ref)

    @pl.when(i == pl.num_programs(0) - 1)
    def _():
        wait_rows(pos_next_ref, next_slot)


def combine_ple(x, route, pos, ys, p, ple_gain, w_ple_gate, w_ple, final_gain=None):
    t, d = x.shape
    tm = MOE_ROWS_STEP
    final = final_gain is not None
    n_steps = t // tm
    in_specs = [pl.BlockSpec((1, 1, 2 * tm), lambda i: (i, 0, 0), memory_space=pltpu.SMEM),
                pl.BlockSpec((1, 1, 2 * tm), lambda i: (jnp.minimum(i + 1, n_steps - 1), 0, 0), memory_space=pltpu.SMEM),
                pl.BlockSpec((tm, d), lambda i: (i, 0)), pl.BlockSpec((tm, LANES), lambda i: (i, 0)),
                pl.BlockSpec((tm, p.shape[1]), lambda i: (i, 0)), pl.BlockSpec((1, d), lambda i: (0, 0)),
                pl.BlockSpec((d, d), lambda i: (0, 0), pipeline_mode=pl.Buffered(1)),
                pl.BlockSpec((p.shape[1], d), lambda i: (0, 0), pipeline_mode=pl.Buffered(1))]
    args = [pos, pos, x, route, p, ple_gain.reshape(1, d), w_ple_gate, w_ple]
    if final:
        in_specs.append(pl.BlockSpec((1, d), lambda i: (0, 0)))
        args.append(final_gain.reshape(1, d))
    in_specs.append(pl.BlockSpec(memory_space=pl.ANY))
    args.append(ys)
    return pl.pallas_call(
        functools.partial(_combine_body, final=final), grid=(n_steps,), in_specs=in_specs,
        out_specs=pl.BlockSpec((tm, d), lambda i: (i, 0)), out_shape=jax.ShapeDtypeStruct((t, d), F32),
        scratch_shapes=[pltpu.VMEM((2, 2, tm, d), F32), pltpu.SemaphoreType.DMA((2,))],
        compiler_params=_cparams(1), name="moe_combine_ple")(*args)


def moe_ple(x, ffn_gain, w_router, wg, wu, wd, p, ple_gain, w_ple_gate, w_ple, final_gain=None):
    t = x.shape[0]
    n_rows_pad = 2 * t + N_EXPERTS * MOE_TILE
    route = moe_route(x, ffn_gain, w_router)
    pos, tile_expert, tile_valid = _row_positions(route, n_rows_pad)
    xs = scatter_rows(x, pos, n_rows_pad)
    ys = grouped_swiglu(xs, ffn_gain, wg, wu, wd, tile_expert, tile_valid)
    return combine_ple(x, route, pos, ys, p, ple_gain, w_ple_gate, w_ple, final_gain)


CONV_PAD = 8


def _causal_conv_silu(xf, w_ref, pad_ref, bias=None):
    k_w = w_ref.shape[0]
    s = xf.shape[0]
    pad_ref[0:CONV_PAD, :] = jnp.zeros((CONV_PAD, xf.shape[1]), F32)
    pad_ref[CONV_PAD:CONV_PAD + s, :] = xf
    y = xf * w_ref[k_w - 1:k_w, :]
    for j in range(k_w - 1):
        start = CONV_PAD - (k_w - 1 - j)
        y = y + pad_ref[start:start + s, :] * w_ref[j:j + 1, :]
    if bias is not None:
        y = y + bias
    return _silu(y)


def _chunk_cumsum(x, rows):
    s = 1
    while s < CHUNK:
        x = x + jnp.where(rows % CHUNK >= s, pltpu.roll(x, s, 0), 0.0)
        s *= 2
    return x


def _tri_iotas():
    return lax.broadcasted_iota(jnp.int32, (CHUNK, CHUNK), 0), lax.broadcasted_iota(jnp.int32, (CHUNK, CHUNK), 1)


def _decay_from(g_col, g_row, incl):
    return jnp.where(incl, jnp.exp(jnp.where(incl, g_col - g_row, 0.0)), 0.0)


def _decay_matrix(gm_b, incl):
    return _decay_from(gm_b, gm_b.T, incl)


def _unit_lower_inverses(mats, ri, ci):
    eye = (ri == ci).astype(F32)
    first = (ri - ci == 1) & (ri % 2 == 1)
    ts = [eye - jnp.where(first, a, 0.0) for a in mats]
    s = 2
    while s < CHUNK:
        off_mask = ((ri // s) % 2 == 1) & (ci // s == ri // s - 1)
        tbs = [t.astype(BF16) for t in ts]
        inner = [_dot(jnp.where(off_mask, a, 0.0).astype(BF16), tb).astype(BF16) for a, tb in zip(mats, tbs)]
        outer = [_dot(tb, x) for tb, x in zip(tbs, inner)]
        ts = [t - x for t, x in zip(ts, outer)]
        s *= 2
    return ts


def _dn_body(alog_ref, dtb_ref, q_ref, k_ref, v_ref, z_ref, sm_ref, cwq_ref, cwk_ref, cwv_ref, ng_ref, o_ref,
             qn_scr, kn_scr, vc_scr, p_scr, n_scr, qp_scr, o0_scr, gl_scr, pad_scr, gall_scr, ball_scr):
    s = q_ref.shape[0]
    hd = DN_HEAD_DIM
    n_chunks = s // CHUNK
    ri, ci = _tri_iotas()
    incl = ri >= ci
    strict = ri > ci

    def l2n(x):
        return x * lax.rsqrt(jnp.sum(x * x, axis=-1, keepdims=True) + RMS_EPS)

    @pl.when(pl.program_id(1) == 0)
    def _():
        rows = lax.broadcasted_iota(jnp.int32, (s, LANES), 0)
        sm = sm_ref[...]
        gall_scr[...] = _chunk_cumsum(-jnp.exp(alog_ref[...]) * _softplus(sm + dtb_ref[...]), rows)
        ball_scr[...] = jax.nn.sigmoid(sm)

    for hh in range(DN_HEADS_PER_STEP):
        h = pl.program_id(1) * DN_HEADS_PER_STEP + hh
        hs = slice(hh * hd, (hh + 1) * hd)
        qn_scr[...] = l2n(_causal_conv_silu(q_ref[:, hs].astype(F32), cwq_ref.at[:, hs], pad_scr)) * hd ** -0.5
        kn_scr[...] = l2n(_causal_conv_silu(k_ref[:, hs].astype(F32), cwk_ref.at[:, hs], pad_scr))
        vc_scr[...] = _causal_conv_silu(v_ref[:, hs].astype(F32), cwv_ref.at[:, hs], pad_scr)

        def factor_group(cg, carry, h=h):
            us = range(DN_UNROLL)
            sls = [pl.ds(pl.multiple_of((cg * DN_UNROLL + u) * CHUNK, CHUNK), CHUNK) for u in us]
            kc = [kn_scr[sl, :] for sl in sls]

            def head_lane(ref, sl, l):
                return jnp.broadcast_to(jnp.sum(jnp.where(ci == l, ref[sl, :], 0.0), axis=1, keepdims=True), (CHUNK, LANES))

            gm = [head_lane(gall_scr, sl, h) for sl in sls]
            bt = [head_lane(ball_scr, sl, DN_HEADS + h) for sl in sls]
            decay = [_decay_matrix(gm[u], incl) for u in us]
            kb = [kc[u] * bt[u] for u in us]
            kcb = [kc[u].astype(BF16) for u in us]
            kk = [_dot_nt(kb[u].astype(BF16), kcb[u]) for u in us]
            qk = [_dot_nt(qn_scr[sls[u], :].astype(BF16), kcb[u]) for u in us]
            t_inv = _unit_lower_inverses([jnp.where(strict, kk[u] * decay[u], 0.0) for u in us], ri, ci)
            eg = [jnp.exp(gm[u]) for u in us]
            rhs = [jnp.concatenate([kb[u] * eg[u], vc_scr[sls[u], :] * bt[u]], axis=1).astype(BF16) for u in us]
            wu = [_dot(t_inv[u].astype(BF16), rhs[u]).astype(BF16) for u in us]
            g_last = [gm[u][CHUNK - 1:CHUNK, :] for u in us]
            pn = [_dot_tn((kc[u] * jnp.exp(g_last[u] - gm[u])).astype(BF16), wu[u]) for u in us]
            qo = [_dot((qk[u] * decay[u]).astype(BF16), wu[u]) for u in us]
            for u in us:
                p_scr[h, sls[u], :] = pn[u][:, :hd].astype(BF16)
                n_scr[h, sls[u], :] = pn[u][:, hd:]
                qp_scr[h, sls[u], :] = (qn_scr[sls[u], :] * eg[u] - qo[u][:, :hd]).astype(BF16)
                o0_scr[h, sls[u], :] = qo[u][:, hd:].astype(BF16)
                gl_scr[h, pl.ds(pl.multiple_of((cg * DN_UNROLL + u) * 8, 8), 8), :] = jnp.broadcast_to(
                    jnp.exp(g_last[u]), (8, LANES))
            return carry

        lax.fori_loop(0, n_chunks // DN_UNROLL, factor_group, 0)

    @pl.when(pl.program_id(1) == pl.num_programs(1) - 1)
    def _():
        def scan(c, states):
            sl = pl.ds(pl.multiple_of(c * CHUNK, CHUNK), CHUNK)
            sbs = [st.astype(BF16) for st in states]
            outs = [_dot(qp_scr[h, sl, :], sbs[h]) for h in range(DN_HEADS)]
            decs = [_dot(p_scr[h, sl, :], sbs[h]) for h in range(DN_HEADS)]
            new = []
            for h in range(DN_HEADS):
                hs = slice(h * hd, (h + 1) * hd)
                e_last = gl_scr[h, pl.ds(pl.multiple_of(c * 8, 8), 1), :]
                new.append(states[h] * e_last - decs[h] + n_scr[h, sl, :])
                o = outs[h] + o0_scr[h, sl, :]
                on = o * lax.rsqrt(jnp.mean(o * o, axis=-1, keepdims=True) + RMS_EPS) * ng_ref[...]
                o_ref[sl, hs] = (on * _silu(z_ref[sl, hs].astype(F32))).astype(BF16)
            return tuple(new)

        lax.fori_loop(0, n_chunks, scan, tuple(jnp.zeros((hd, hd), F32) for _ in range(DN_HEADS)))


def gated_deltanet(pr_dn, small, conv_w, a_log, dt_bias, norm_g, bsz, s):
    t = bsz * s
    hd = DN_HEAD_DIM
    n_steps = DN_HEADS // DN_HEADS_PER_STEP
    assert (s // CHUNK) % DN_UNROLL == 0

    def col(j0):
        return pl.BlockSpec((s, DN_STEP_W), lambda b, h, j0=j0: (b, j0 + h))

    def cw(j0):
        return pl.BlockSpec((CONV_WIDTH, DN_STEP_W), lambda b, h, j0=j0: (0, j0 + h))

    def per_head(dtype):
        return pltpu.VMEM((DN_HEADS, s, hd), dtype)

    def lanes(v):
        return jnp.zeros((1, LANES), F32).at[0, :DN_HEADS].set(v)

    lane_vec = pl.BlockSpec((1, LANES), lambda b, h: (0, 0))
    return pl.pallas_call(
        _dn_body, grid=(bsz, n_steps),
        in_specs=[lane_vec, lane_vec, col(0), col(n_steps), col(2 * n_steps),
                  pl.BlockSpec((s, DN_WIDTH), lambda b, h: (b, 3), pipeline_mode=pl.Buffered(1)),
                  pl.BlockSpec((s, LANES), lambda b, h: (b, 0)),
                  cw(0), cw(n_steps), cw(2 * n_steps), pl.BlockSpec((1, hd), lambda b, h: (0, 0))],
        out_specs=pl.BlockSpec((s, DN_WIDTH), lambda b, h: (b, 0)),
        out_shape=jax.ShapeDtypeStruct((t, DN_WIDTH), BF16),
        scratch_shapes=[pltpu.VMEM((s, hd), F32)] * 3 + [per_head(BF16), per_head(F32), per_head(BF16), per_head(BF16),
                                                        pltpu.VMEM((DN_HEADS, s // CHUNK * 8, LANES), F32),
                                                        pltpu.VMEM((CONV_PAD + s, hd), F32),
                                                        pltpu.VMEM((s, LANES), F32), pltpu.VMEM((s, LANES), F32)],
        compiler_params=_cparams(2), name="gated_deltanet")(
            lanes(a_log), lanes(dt_bias), pr_dn, pr_dn, pr_dn, pr_dn, small, conv_w, conv_w, conv_w, norm_g.reshape(1, hd))


def _expand_heads(v, g, n_rows):
    lane_head = lax.broadcasted_iota(jnp.int32, (n_rows, SSM_GW), 1) // SSM_HEAD_DIM
    out = jnp.zeros((n_rows, SSM_GW), F32)
    for hh in range(SSM_HPG):
        l0 = DT_LANE0 + g * SSM_HPG + hh
        out = jnp.where(lane_head == hh, v[:, l0:l0 + 1], out)
    return out


def _ssd_body(pr_ref, sm_ref, cw_ref, cb_ref, alog_ref, dtb_ref, d_ref, ng_ref, o_ref,
              x_scr, b_scr, c_scr, dt_scr, acs_scr, st_scr, pad_scr):
    s = pr_ref.shape[0]
    rows = lax.broadcasted_iota(jnp.int32, (s, LANES), 0)
    n_state = SSM_GROUPS * SSM_STATE
    k_w = cw_ref.shape[0]
    nr = SSM_CONV_ROWS

    def conv_rows(i, carry):
        rsl = pl.ds(pl.multiple_of(i * nr, nr), nr)
        for j in range(SSM_XBC // LANES):
            cs = slice(j * LANES, (j + 1) * LANES)
            src = j * LANES if j * LANES < SSM_WIDTH else j * LANES + SSM_WIDTH
            xf = pr_ref[rsl, src:src + LANES].astype(F32)
            pad_scr[0:CONV_PAD, cs] = pad_scr[nr:nr + CONV_PAD, cs]
            pad_scr[CONV_PAD:CONV_PAD + nr, cs] = xf
            y = xf * cw_ref[k_w - 1:k_w, cs] + cb_ref[:, cs]
            for tap in range(k_w - 1):
                start = CONV_PAD - (k_w - 1 - tap)
                y = y + pad_scr[start:start + nr, cs] * cw_ref[tap:tap + 1, cs]
            y = _silu(y)
            if j * LANES < SSM_WIDTH:
                x_scr[rsl, cs] = y
            elif j * LANES < SSM_WIDTH + n_state:
                b_scr[rsl, j * LANES - SSM_WIDTH:(j + 1) * LANES - SSM_WIDTH] = y.astype(BF16)
            else:
                c_scr[rsl, j * LANES - SSM_WIDTH - n_state:(j + 1) * LANES - SSM_WIDTH - n_state] = y.astype(BF16)
        return carry

    pad_scr[nr:nr + CONV_PAD, :] = jnp.zeros((CONV_PAD, SSM_XBC), F32)
    lax.fori_loop(0, s // nr, conv_rows, 0)
    dt = _softplus(sm_ref[...] + dtb_ref[...])
    dt_scr[...] = dt
    acs_scr[...] = _chunk_cumsum(dt * (-jnp.exp(alog_ref[...])), rows)
    st_scr[...] = jnp.zeros_like(st_scr)

    ri, ci = _tri_iotas()
    incl = ri >= ci

    groups = range(SSM_GROUPS)

    def chunk_terms(c):
        sl = pl.ds(pl.multiple_of(c * CHUNK, CHUNK), CHUNK)
        dt_c, acs_c = dt_scr[sl, :], acs_scr[sl, :]
        acs_last = acs_c[CHUNK - 1:CHUNK, :]
        acs_t = acs_c.T
        bc = [b_scr[sl, g * SSM_STATE:(g + 1) * SSM_STATE] for g in groups]
        cc = [c_scr[sl, g * SSM_STATE:(g + 1) * SSM_STATE] for g in groups]
        xg = [x_scr[sl, g * SSM_GW:(g + 1) * SSM_GW] for g in groups]
        xdt = [xg[g] * _expand_heads(dt_c, g, CHUNK) for g in groups]
        acs_e = [_expand_heads(acs_c, g, CHUNK) for g in groups]
        last_e = [_expand_heads(acs_last, g, 1) for g in groups]
        cbm = [_dot_nt(cc[g], bc[g]) for g in groups]
        y_diag = []
        for g in groups:
            xdt_b = xdt[g].astype(BF16)
            for hh in range(SSM_HPG):
                l0 = DT_LANE0 + g * SSM_HPG + hh
                lm = _decay_from(jnp.broadcast_to(acs_c[:, l0:l0 + 1], (CHUNK, CHUNK)),
                                 jnp.broadcast_to(acs_t[l0:l0 + 1, :], (CHUNK, CHUNK)), incl)
                y_diag.append(_dot((cbm[g] * lm).astype(BF16), xdt_b[:, hh * SSM_HEAD_DIM:(hh + 1) * SSM_HEAD_DIM]))
        st_new = [_dot_tn(bc[g], (xdt[g] * jnp.exp(last_e[g] - acs_e[g])).astype(BF16)) for g in groups]
        y_local = [jnp.concatenate(y_diag[g * SSM_HPG:(g + 1) * SSM_HPG], axis=1) + _expand_heads(d_ref[...], g, 1) * xg[g]
                   for g in groups]
        return sl, cc, acs_e, last_e, st_new, y_local

    def chunks(it, carry):
        terms = [chunk_terms(it * SSM_LOCKSTEP + u) for u in range(SSM_LOCKSTEP)]
        state = [st_scr[g] for g in groups]
        for sl, cc, acs_e, last_e, st_new, y_local in terms:
            y_off = [_dot(cc[g], state[g].astype(BF16)) for g in groups]
            state = [state[g] * jnp.exp(last_e[g]) + st_new[g] for g in groups]
            for g in groups:
                gs = slice(g * SSM_GW, (g + 1) * SSM_GW)
                y = y_off[g] * jnp.exp(acs_e[g]) + y_local[g]
                y = y * _silu(pr_ref[sl, SSM_WIDTH + g * SSM_GW:SSM_WIDTH + (g + 1) * SSM_GW].astype(F32))
                y = y * lax.rsqrt(jnp.mean(y * y, axis=-1, keepdims=True) + RMS_EPS) * ng_ref[:, gs]
                o_ref[sl, gs] = y.astype(BF16)
        for g in groups:
            st_scr[g] = state[g]
        return carry

    lax.fori_loop(0, s // CHUNK // SSM_LOCKSTEP, chunks, 0)


def mamba2_ssd(pr_ssm, small, conv_w, conv_b, a_log, dt_bias, d_skip, norm_g, bsz, s):
    t = bsz * s

    def lanes(v):
        return jnp.zeros((1, LANES), F32).at[0, DT_LANE0:DT_LANE0 + SSM_HEADS].set(v)

    def whole(shape):
        return pl.BlockSpec(shape, lambda b: (0, 0))

    n_state = SSM_GROUPS * SSM_STATE
    return pl.pallas_call(
        _ssd_body, grid=(bsz,),
        in_specs=[pl.BlockSpec((s, pr_ssm.shape[1]), lambda b: (b, 0)),
                  pl.BlockSpec((s, LANES), lambda b: (b, 0)),
                  whole((CONV_WIDTH, SSM_XBC)), whole((1, SSM_XBC)), whole((1, LANES)), whole((1, LANES)), whole((1, LANES)),
                  whole((1, SSM_WIDTH))],
        out_specs=pl.BlockSpec((s, SSM_WIDTH), lambda b: (b, 0)),
        out_shape=jax.ShapeDtypeStruct((t, SSM_WIDTH), BF16),
        scratch_shapes=[pltpu.VMEM((s, SSM_WIDTH), F32), pltpu.VMEM((s, n_state), BF16), pltpu.VMEM((s, n_state), BF16),
                        pltpu.VMEM((s, LANES), F32), pltpu.VMEM((s, LANES), F32),
                        pltpu.VMEM((SSM_GROUPS, SSM_STATE, SSM_GW), F32),
                        pltpu.VMEM((CONV_PAD + SSM_CONV_ROWS, SSM_XBC), F32)],
        compiler_params=_cparams(1), name="mamba2_ssd")(
            pr_ssm, small, conv_w, conv_b.reshape(1, -1), lanes(a_log), lanes(dt_bias), lanes(d_skip), norm_g.reshape(1, -1))


def _attn_body(pr_ref, o_ref, qf, kf, vf, og, lg):
    s = pr_ref.shape[0]
    c = ATTN_BLOCK
    e = ATTN_HEAD_DIM
    ii = lax.broadcasted_iota(jnp.int32, (c, c), 0)
    jj = lax.broadcasted_iota(jnp.int32, (c, c), 1)
    lane_head = lax.broadcasted_iota(jnp.int32, (c, LANES), 1) // e
    for gi, (window, dil) in enumerate(DILATION_GROUPS):
        w_sub = window // dil
        nb = s // dil // c
        for j in range(HALVES):
            c0 = gi * ATTN_OUT_WIDTH + j * LANES
            qf[j] = pr_ref[:, c0:c0 + LANES].astype(F32) * e ** -0.5
            kf[j] = pr_ref[:, ATTN_QKV + c0:ATTN_QKV + c0 + LANES].astype(F32)
            vf[j] = pr_ref[:, 2 * ATTN_QKV + c0:2 * ATTN_QKV + c0 + LANES].astype(F32)
        d_cur = ii - jj
        d_prev = d_cur + c
        ok_cur = (d_cur >= 0) & (d_cur <= w_sub)
        ok_prev_static = d_prev <= w_sub

        def blocks(it, carry, gi=gi, dil=dil, nb=nb, d_cur=d_cur, d_prev=d_prev, ok_cur=ok_cur, ok_prev_static=ok_prev_static):
            cur, prev, ok_prev = [], [], []
            for b in range(ATTN_LOCKSTEP):
                idx = it * ATTN_LOCKSTEP + b
                r, n = idx // nb, idx % nb
                cur.append(pl.ds(r + n * (c * dil), c, stride=dil))
                prev.append(pl.ds(r + jnp.maximum(n - 1, 0) * (c * dil), c, stride=dil))
                ok_prev.append(ok_prev_static & (n > 0))
            units = [(b, j, hh) for b in range(ATTN_LOCKSTEP) for j in range(HALVES) for hh in range(HEADS_PER_HALF)]
            def values(rows_):
                return [[[jnp.where(lane_head == hh, vf.at[j][rows_[b], :], 1.0).astype(BF16) for hh in range(HEADS_PER_HALF)]
                         for j in range(HALVES)] for b in range(ATTN_LOCKSTEP)]

            vcur, vprev = values(cur), values(prev)
            s_cur, s_prev = [], []
            for b in range(ATTN_LOCKSTEP):
                for j in range(HALVES):
                    q = qf.at[j][cur[b], :]
                    kc, kp = kf.at[j][cur[b], :].astype(BF16), kf.at[j][prev[b], :].astype(BF16)
                    for hh in range(HEADS_PER_HALF):
                        qh = jnp.where(lane_head == hh, q, 0.0).astype(BF16)
                        s_cur.append(_dot_nt(qh, kc))
                        s_prev.append(_dot_nt(qh, kp))
            probs, maxes = [], []
            for i, (b, j, hh) in enumerate(units):
                head = gi * ATTN_GROUP_HEADS + j * HEADS_PER_HALF + hh
                slope = float(2.0 ** (-ALIBI_MAX_BIAS * (head + 1.0) / ATTN_HEADS)) * dil
                sc = jnp.where(ok_cur, s_cur[i] - slope * d_cur.astype(F32), NEG_BIG)
                sp = jnp.where(ok_prev[b], s_prev[i] - slope * d_prev.astype(F32), NEG_BIG)
                m = jnp.max(jnp.maximum(sc, sp), axis=1, keepdims=True)
                probs.append((jnp.exp(sc - m).astype(BF16), jnp.exp(sp - m).astype(BF16)))
                maxes.append(m)
            pv = [_dot(probs[i][0], vcur[b][j][hh]) + _dot(probs[i][1], vprev[b][j][hh]) for i, (b, j, hh) in enumerate(units)]
            for b in range(ATTN_LOCKSTEP):
                for j in range(HALVES):
                    i0 = (b * HALVES + j) * HEADS_PER_HALF
                    first = lane_head == 0
                    num = jnp.where(first, pv[i0], pv[i0 + 1])
                    den = pltpu.roll(jnp.where(first, pv[i0 + 1], pv[i0]), e, 1)
                    og.at[gi, j][cur[b], :] = num / den
                    lg.at[gi, j][cur[b], :] = jnp.where(first, maxes[i0], maxes[i0 + 1]) + jnp.log(den)
            return carry

        lax.fori_loop(0, dil * nb // ATTN_LOCKSTEP, blocks, 0)
    for j in range(HALVES):
        m = jnp.maximum(jnp.maximum(lg[0, j], lg[1, j]), lg[2, j])
        num = jnp.zeros((s, LANES), F32)
        den = jnp.zeros((s, LANES), F32)
        for gi in range(N_DIL):
            w = jnp.exp(lg[gi, j] - m)
            num = num + w * og[gi, j]
            den = den + w
        o_ref[:, j * LANES:(j + 1) * LANES] = (num / den).astype(BF16)


def dilated_attention(pr_attn, bsz, s):
    t = bsz * s
    assert N_DIL == 3 and HEADS_PER_HALF == 2
    for window, dil in DILATION_GROUPS:
        assert s % (dil * ATTN_BLOCK) == 0 and window // dil <= ATTN_BLOCK
    return pl.pallas_call(
        _attn_body, grid=(bsz,),
        in_specs=[pl.BlockSpec((s, pr_attn.shape[1]), lambda b: (b, 0))],
        out_specs=pl.BlockSpec((s, ATTN_OUT_WIDTH), lambda b: (b, 0)),
        out_shape=jax.ShapeDtypeStruct((t, ATTN_OUT_WIDTH), BF16),
        scratch_shapes=[pltpu.VMEM((HALVES, s, LANES), F32)] * 3 + [pltpu.VMEM((N_DIL, HALVES, s, LANES), F32)] * 2,
        compiler_params=_cparams(1), name="dilated_attention")(pr_attn)


def _split_w_in(w_in):
    cuts = np.cumsum(IN_SPLIT_SIZES)[:-1].tolist()
    dn_qkv, dn_z, dn_a, dn_b, ssm_xbc, ssm_z, ssm_dt, attn_qkv, gate = jnp.split(w_in, cuts, axis=-1)
    w_dn = jnp.concatenate([dn_qkv, dn_z], axis=1).astype(BF16)
    w_ssm = jnp.concatenate([ssm_xbc[:, :SSM_WIDTH], ssm_z, ssm_xbc[:, SSM_WIDTH:]], axis=1).astype(BF16)
    small = jnp.concatenate([dn_a, dn_b, ssm_dt], axis=1)
    w_small = jnp.pad(small, ((0, 0), (0, LANES - small.shape[1]))).astype(BF16)
    return w_dn, w_ssm, attn_qkv.astype(BF16), gate.astype(BF16), w_small


def kernel(x, p, mix_norm, w_in, dn_conv, dn_a_log, dn_dt_bias, dn_norm, ssm_conv, ssm_conv_b, ssm_a_log, ssm_dt_bias, ssm_d, ssm_norm, w_br_dn, w_br_ssm, w_br_attn, w_out, ffn_norm, w_ff_gate, w_ff_up, w_ff_down, w_router, w_moe_gate, w_moe_up, w_moe_down, ple_norm, w_ple, w_ple_gate, final_norm):
    bsz, s, d = x.shape
    t = bsz * s
    depth = w_in.shape[0]
    x = x.reshape(t, d)
    for i in range(depth):
        w_dn, w_ssm, w_attn, w_gate, w_small = _split_w_in(w_in[i])
        g = mix_norm[i]
        pr_dn = norm_matmul(x, g, w_dn, 512)
        pr_ssm, small = norm_matmul(x, g, w_ssm, 512, w_small=w_small)
        pr_attn = norm_matmul(x, g, w_attn, 768)

        y_dn = gated_deltanet(pr_dn, small, dn_conv[i], dn_a_log[i], dn_dt_bias[i], dn_norm[i], bsz, s)
        y_ssm = mamba2_ssd(pr_ssm, small, ssm_conv[i], ssm_conv_b[i], ssm_a_log[i], ssm_dt_bias[i], ssm_d[i], ssm_norm[i], bsz, s)
        y_attn = dilated_attention(pr_attn, bsz, s)

        x = merge_out(x, g, w_gate, y_dn, y_ssm, y_attn,
                      w_br_dn[i].astype(BF16), w_br_ssm[i].astype(BF16), w_br_attn[i].astype(BF16), w_out[i].astype(BF16))

        j = i // 2
        fin = final_norm if i == depth - 1 else None
        common = (p[i].reshape(t, -1), ple_norm[i], w_ple_gate[i].astype(BF16), w_ple[i].astype(BF16))
        if i % 2 == 0:
            x = ffn_ple(x, ffn_norm[i], w_ff_gate[j][None].astype(BF16), w_ff_up[j][None].astype(BF16),
                        w_ff_down[j][None].astype(BF16), *common, tm=1024, sub=256, final_gain=fin)
        else:
            wr = jnp.pad(w_router[j], ((0, 0), (0, LANES - N_EXPERTS)))
            x = moe_ple(x, ffn_norm[i], wr, w_moe_gate[j].astype(BF16), w_moe_up[j].astype(BF16), w_moe_down[j].astype(BF16),
                        *common, final_gain=fin)
    return x.reshape(bsz, s, d)
```
